```python
import jax, jax.numpy as jnp
from jax import lax
import numpy as np

D_MODEL = 2048
BATCH = 1
SEQ = 16384
DEPTH = 1
DEC_BATCH = 8
DEC_SEQ = 16
PAST_LEN = 1024

CHUNK = 64
MLP_CHUNK = 128
D_MIX = D_MODEL
D_A = D_MIX // 2
D_B = D_MIX - D_A
N_HEADS_A = 8
HEAD_DIM_A = D_A // N_HEADS_A
CONV_WIDTH = 31
D_FF = 4 * D_MODEL
D_PLE = 256
EPS = 1e-6

kernel_name = "hybrid_chunk_gmlp_conformer_conv_step"


def rmsnorm(x, g):
    xf = x.astype(jnp.float32)
    y = xf * lax.rsqrt(jnp.mean(xf * xf, axis=-1, keepdims=True) + EPS)
    return (y * g.astype(jnp.float32)).astype(x.dtype)


def layernorm(x, g, b):
    xf = x.astype(jnp.float32)
    mu = jnp.mean(xf, axis=-1, keepdims=True)
    xc = xf - mu
    var = jnp.mean(xc * xc, axis=-1, keepdims=True)
    y = xc * lax.rsqrt(var + EPS) * g.astype(jnp.float32) + b.astype(jnp.float32)
    return y.astype(x.dtype)


def masked_spatial_weights(w_s):
    i = jnp.arange(MLP_CHUNK)
    mask = (i[None, :] // CHUNK) <= (i[:, None] // CHUNK)
    return jnp.where(mask[None], w_s, jnp.zeros_like(w_s))


def spatial_mix_prompt(v, w_s, b_s):
    B, S, _ = v.shape
    vc = v.reshape(B, S // MLP_CHUNK, MLP_CHUNK, N_HEADS_A, HEAD_DIM_A)
    w = masked_spatial_weights(w_s)
    out = jnp.einsum('hij,bnjhd->bnihd', w, vc) + b_s.T[None, None, :, :, None]
    return out.reshape(B, S, D_A)


def spatial_mix_sample(v, w_s, b_s):
    B, T, _ = v.shape
    vh = v.reshape(B, T, N_HEADS_A, HEAD_DIM_A)
    w = masked_spatial_weights(w_s)[:, :T, :T]
    out = jnp.einsum('hij,bjhd->bihd', w, vh) + b_s[:, :T].T[None, :, :, None]
    return out.reshape(B, T, D_A)


def causal_depthwise_conv(xpad, w_dw, b_dw):
    C = xpad.shape[-1]
    y = lax.conv_general_dilated(xpad, w_dw[:, None, :], window_strides=(1,), padding='VALID',
                                 dimension_numbers=('NWC', 'WIO', 'NWC'), feature_group_count=C)
    return y + b_dw


def trunk_layer(x, p, conv_hist, is_prompt, g_mix, w_in, b_in, g_v, b_v, w_s, b_s, w_dw, b_dw,
                g_c, b_c, g_oa, g_ob, w_out, g_ffn, w1, w2, g_ple, w_pg, w_pe):
    h = rmsnorm(x, g_mix)
    z = h @ w_in + b_in
    za = jax.nn.gelu(z[..., :2 * D_A])
    u, v = za[..., :D_A], za[..., D_A:]
    v = layernorm(v, g_v, b_v)
    mixed = spatial_mix_prompt(v, w_s, b_s) if is_prompt else spatial_mix_sample(v, w_s, b_s)
    y_a = u * mixed
    zb = z[..., 2 * D_A:]
    glu = zb[..., :D_B] * jax.nn.sigmoid(zb[..., D_B:])
    xpad = jnp.concatenate([conv_hist.astype(glu.dtype), glu], axis=1)
    new_hist = xpad[:, -(CONV_WIDTH - 1):]
    c = causal_depthwise_conv(xpad, w_dw, b_dw)
    y_b = jax.nn.silu(layernorm(c, g_c, b_c))
    o = jnp.concatenate([rmsnorm(y_a, g_oa), rmsnorm(y_b, g_ob)], axis=-1) @ w_out
    x = x + o
    f = jnp.square(jax.nn.relu(rmsnorm(x, g_ffn) @ w1)) @ w2
    x = x + f
    gate = jax.nn.sigmoid(rmsnorm(x, g_ple) @ w_pg)
    x = x + (p @ w_pe) * gate
    return x, new_hist, v


def setup_inputs(seed: int = 0) -> dict:
    key = jax.random.key(seed)
    ks = jax.random.split(key, 32)
    f32 = jnp.float32

    def nrm(k, shape, scale):
        return jax.random.normal(k, shape, f32) * scale

    def gain(k, shape):
        return 1.0 + 0.05 * jax.random.normal(k, shape, f32)

    return {
        "x_prompt": nrm(ks[0], (BATCH, SEQ, D_MODEL), 1.0),
        "x_sample": nrm(ks[1], (DEC_BATCH, DEC_SEQ, D_MODEL), 1.0),
        "p_prompt": nrm(ks[2], (DEPTH, BATCH, SEQ, D_PLE), 1.0),
        "p_sample": nrm(ks[3], (DEPTH, DEC_BATCH, DEC_SEQ, D_PLE), 1.0),
        "cache_conv": nrm(ks[4], (DEPTH, DEC_BATCH, CONV_WIDTH - 1, D_B), 0.5),
        "g_mix": gain(ks[5], (DEPTH, D_MODEL)),
        "w_in": nrm(ks[6], (DEPTH, D_MODEL, 2 * D_A + 2 * D_B), D_MODEL ** -0.5),
        "b_in": nrm(ks[7], (DEPTH, 2 * D_A + 2 * D_B), 0.02),
        "g_v": gain(ks[8], (DEPTH, D_A)),
        "b_v": nrm(ks[9], (DEPTH, D_A), 0.02),
        "w_s": nrm(ks[10], (DEPTH, N_HEADS_A, MLP_CHUNK, MLP_CHUNK), MLP_CHUNK ** -0.5),
        "b_s": gain(ks[11], (DEPTH, N_HEADS_A, MLP_CHUNK)),
        "w_dw": nrm(ks[12], (DEPTH, CONV_WIDTH, D_B), CONV_WIDTH ** -0.5),
        "b_dw": nrm(ks[13], (DEPTH, D_B), 0.02),
        "g_c": gain(ks[14], (DEPTH, D_B)),
        "b_c": nrm(ks[15], (DEPTH, D_B), 0.02),
        "g_oa": gain(ks[16], (DEPTH, D_A)),
        "g_ob": gain(ks[17], (DEPTH, D_B)),
        "w_out": nrm(ks[18], (DEPTH, D_MIX, D_MODEL), D_MIX ** -0.5),
        "g_ffn": gain(ks[19], (DEPTH, D_MODEL)),
        "w1": nrm(ks[20], (DEPTH, D_MODEL, D_FF), D_MODEL ** -0.5),
        "w2": nrm(ks[21], (DEPTH, D_FF, D_MODEL), D_FF ** -0.5),
        "g_ple": gain(ks[22], (DEPTH, D_MODEL)),
        "w_pg": nrm(ks[23], (DEPTH, D_MODEL, D_MODEL), D_MODEL ** -0.5),
        "w_pe": nrm(ks[24], (DEPTH, D_PLE, D_MODEL), D_PLE ** -0.5),
        "g_final": gain(ks[25], (D_MODEL,)),
    }


def reference(x_prompt, x_sample, p_prompt, p_sample, cache_conv, g_mix, w_in, b_in, g_v, b_v,
              w_s, b_s, w_dw, b_dw, g_c, b_c, g_oa, g_ob, w_out, g_ffn, w1, w2, g_ple, w_pg,
              w_pe, g_final):
    xp, xs = x_prompt, x_sample
    conv_p_list, conv_s_list, v_s_list = [], [], []
    for i in range(DEPTH):
        lw = (g_mix[i], w_in[i], b_in[i], g_v[i], b_v[i], w_s[i], b_s[i], w_dw[i], b_dw[i],
              g_c[i], b_c[i], g_oa[i], g_ob[i], w_out[i], g_ffn[i], w1[i], w2[i], g_ple[i],
              w_pg[i], w_pe[i])
        hist0 = jnp.zeros((xp.shape[0], CONV_WIDTH - 1, D_B), xp.dtype)
        xp, hist_p, _ = trunk_layer(xp, p_prompt[i], hist0, True, *lw)
        xs, hist_s, v_s = trunk_layer(xs, p_sample[i], cache_conv[i], False, *lw)
        conv_p_list.append(hist_p)
        conv_s_list.append(hist_s)
        v_s_list.append(v_s)
    y_prompt = rmsnorm(xp, g_final)
    y_sample = rmsnorm(xs, g_final)
    new_conv_prompt = jnp.stack(conv_p_list, axis=0)
    new_conv_sample = jnp.stack(conv_s_list, axis=0)
    new_v_sample = jnp.stack(v_s_list, axis=0)
    return (y_prompt, y_sample, new_conv_prompt, new_conv_sample, new_v_sample)
```

```python
import functools

import jax
import jax.numpy as jnp
from jax import lax
from jax.experimental import pallas as pl
from jax.experimental.pallas import tpu as pltpu

D_MODEL = 2048
D_A = 1024
D_B = 1024
N_HEADS_A = 8
HEAD_DIM_A = D_A // N_HEADS_A
MLP_CHUNK = 128
CHUNK = 64
CONV_WIDTH = 31
D_FF = 4 * D_MODEL
D_PLE = 256
EPS = 1e-6

SUBLANES = 8
HIST = 32
HIST_PAD = HIST - (CONV_WIDTH - 1)
VMEM_LIMIT_BYTES = 56 * 1024 * 1024

MIX_ROWS = 256
FFN_ROWS = 512
FFN_COLS = 1024
PLE_ROWS = 512

f32 = jnp.float32
bf16 = jnp.bfloat16


def _rms(x, g):
    return x * lax.rsqrt(jnp.mean(x * x, axis=-1, keepdims=True) + EPS) * g


def _ln(x, g, b):
    mu = jnp.mean(x, axis=-1, keepdims=True)
    xc = x - mu
    var = jnp.mean(xc * xc, axis=-1, keepdims=True)
    return xc * lax.rsqrt(var + EPS) * g + b


def _mix_kernel(x_ref, hist_in_ref, g_mix_ref, w_in_ref, b_in_ref, g_v_ref, b_v_ref, ws_ref, bs_ref,
                w_dw_ref, b_dw_ref, g_c_ref, b_c_ref, g_oa_ref, g_ob_ref, w_out_ref,
                *rest, streams, mask_ws, emit_v):
    if emit_v:
        o_ref, hist_out_ref, v_ref, gpad_ref = rest
    else:
        o_ref, hist_out_ref, gpad_ref = rest
    rows = x_ref.shape[0]
    seg = rows // streams

    @pl.when(pl.program_id(0) == 0)
    def _():
        gpad_ref[:, 0:HIST, :] = hist_in_ref[...]

    x = x_ref[...]
    h = _rms(x, g_mix_ref[...]).astype(bf16)
    z = jnp.dot(h, w_in_ref[...], preferred_element_type=f32) + b_in_ref[...]

    za = jax.nn.gelu(z[:, :2 * D_A])
    u = za[:, :D_A]
    v = _ln(za[:, D_A:], g_v_ref[...], b_v_ref[...])
    if emit_v:
        v_ref[...] = v
    ws = ws_ref[...]
    if mask_ws:
        qi = lax.broadcasted_iota(jnp.int32, (MLP_CHUNK, MLP_CHUNK), 0)
        ki = lax.broadcasted_iota(jnp.int32, (MLP_CHUNK, MLP_CHUNK), 1)
        keep = (ki // CHUNK) <= (qi // CHUNK)
        ws = jnp.where(keep[None], ws, jnp.zeros_like(ws))
    ws = ws.astype(bf16)
    vb = v.astype(bf16)
    mixed_chunks = []
    for c in range(rows // MLP_CHUNK):
        r0 = c * MLP_CHUNK
        heads = [
            jnp.dot(ws[hd], vb[r0:r0 + MLP_CHUNK, hd * HEAD_DIM_A:(hd + 1) * HEAD_DIM_A],
                    preferred_element_type=f32)
            for hd in range(N_HEADS_A)
        ]
        mixed_chunks.append(jnp.concatenate(heads, axis=1) + bs_ref[...])
    mixed = mixed_chunks[0] if len(mixed_chunks) == 1 else jnp.concatenate(mixed_chunks, axis=0)
    y_a = u * mixed

    zb = z[:, 2 * D_A:]
    glu = zb[:, :D_B] * jax.nn.sigmoid(zb[:, D_B:])
    gpad_ref[:, HIST:HIST + seg, :] = glu.reshape(streams, seg, D_B)
    c = jnp.broadcast_to(b_dw_ref[...], (rows, D_B))
    for k in range(CONV_WIDTH):
        tap = gpad_ref[:, HIST_PAD + k:HIST_PAD + k + seg, :].reshape(rows, D_B)
        c = c + w_dw_ref[k:k + 1, :] * tap
    tail = gpad_ref[:, seg:seg + HIST, :]
    gpad_ref[:, 0:HIST, :] = tail
    hist_out_ref[...] = tail
    y_b = jax.nn.silu(_ln(c, g_c_ref[...], b_c_ref[...]))

    merged = jnp.concatenate([_rms(y_a, g_oa_ref[...]), _rms(y_b, g_ob_ref[...])], axis=1).astype(bf16)
    o_ref[...] = x + jnp.dot(merged, w_out_ref[...], preferred_element_type=f32)


def _const_spec(shape):
    zeros = (0,) * len(shape)
    return pl.BlockSpec(shape, lambda *_: zeros, pipeline_mode=pl.Buffered(1))


def _mix_call(x, hist_in, ws, bs, lw, *, rows, streams, mask_ws, emit_v):
    n = x.shape[0]
    seg = rows // streams
    row_spec = lambda cols: pl.BlockSpec((rows, cols), lambda i: (i, 0))
    vec = lambda a: a.reshape(1, -1)
    out_shape = [jax.ShapeDtypeStruct((n, D_MODEL), f32),
                 jax.ShapeDtypeStruct((streams, HIST, D_B), f32)]
    out_specs = [row_spec(D_MODEL), _const_spec((streams, HIST, D_B))]
    if emit_v:
        out_shape.append(jax.ShapeDtypeStruct((n, D_A), f32))
        out_specs.append(row_spec(D_A))
    operands = [x, hist_in, vec(lw["g_mix"]), lw["w_in"], vec(lw["b_in"]), vec(lw["g_v"]), vec(lw["b_v"]),
                ws, bs, lw["w_dw"], vec(lw["b_dw"]), vec(lw["g_c"]), vec(lw["b_c"]),
                vec(lw["g_oa"]), vec(lw["g_ob"]), lw["w_out"]]
    in_specs = [row_spec(D_MODEL)] + [_const_spec(a.shape) for a in operands[1:]]
    return pl.pallas_call(
        functools.partial(_mix_kernel, streams=streams, mask_ws=mask_ws, emit_v=emit_v),
        grid=(n // rows,),
        in_specs=in_specs,
        out_specs=out_specs,
        out_shape=out_shape,
        scratch_shapes=[pltpu.VMEM((streams, HIST + seg, D_B), f32)],
        compiler_params=pltpu.CompilerParams(dimension_semantics=("arbitrary",),
                                             vmem_limit_bytes=VMEM_LIMIT_BYTES),
        name="mix_sample" if emit_v else "mix_prompt",
    )(*operands)


def _ffn_kernel(x_ref, g_ref, w1_ref, w2_ref, o_ref, hn_ref, acc_ref):
    f = pl.program_id(1)

    @pl.when(f == 0)
    def _():
        hn_ref[...] = _rms(x_ref[...], g_ref[...]).astype(bf16)
        acc_ref[...] = jnp.zeros_like(acc_ref)

    h = jnp.dot(hn_ref[...], w1_ref[...], preferred_element_type=f32)
    h = jnp.square(jnp.maximum(h, 0.0)).astype(bf16)
    acc_ref[...] += jnp.dot(h, w2_ref[...], preferred_element_type=f32)

    @pl.when(f == pl.num_programs(1) - 1)
    def _():
        o_ref[...] = x_ref[...] + acc_ref[...]


def _ffn_call(x, g_ffn, w1, w2, *, rows, name):
    n = x.shape[0]
    return pl.pallas_call(
        _ffn_kernel,
        grid=(n // rows, D_FF // FFN_COLS),
        in_specs=[pl.BlockSpec((rows, D_MODEL), lambda i, f: (i, 0)),
                  _const_spec((1, D_MODEL)),
                  pl.BlockSpec((D_MODEL, FFN_COLS), lambda i, f: (0, f)),
                  pl.BlockSpec((FFN_COLS, D_MODEL), lambda i, f: (f, 0))],
        out_specs=pl.BlockSpec((rows, D_MODEL), lambda i, f: (i, 0)),
        out_shape=jax.ShapeDtypeStruct((n, D_MODEL), f32),
        scratch_shapes=[pltpu.VMEM((rows, D_MODEL), bf16), pltpu.VMEM((rows, D_MODEL), f32)],
        compiler_params=pltpu.CompilerParams(dimension_semantics=("arbitrary", "arbitrary"),
                                             vmem_limit_bytes=VMEM_LIMIT_BYTES),
        name=name,
    )(x, g_ffn.reshape(1, -1), w1, w2)


def _ple_kernel(x_ref, p_ref, g_ple_ref, w_pg_ref, w_pe_ref, g_final_ref, o_ref):
    x = x_ref[...]
    gate = jax.nn.sigmoid(jnp.dot(_rms(x, g_ple_ref[...]).astype(bf16), w_pg_ref[...],
                                  preferred_element_type=f32))
    emb = jnp.dot(p_ref[...].astype(bf16), w_pe_ref[...], preferred_element_type=f32)
    o_ref[...] = _rms(x + emb * gate, g_final_ref[...])


def _ple_call(x, p, g_ple, w_pg, w_pe, g_final, *, rows, name):
    n = x.shape[0]
    return pl.pallas_call(
        _ple_kernel,
        grid=(n // rows,),
        in_specs=[pl.BlockSpec((rows, D_MODEL), lambda i: (i, 0)),
                  pl.BlockSpec((rows, D_PLE), lambda i: (i, 0)),
                  _const_spec((1, D_MODEL)),
                  _const_spec((D_MODEL, D_MODEL)),
                  _const_spec((D_PLE, D_MODEL)),
                  _const_spec((1, D_MODEL))],
        out_specs=pl.BlockSpec((rows, D_MODEL), lambda i: (i, 0)),
        out_shape=jax.ShapeDtypeStruct((n, D_MODEL), f32),
        compiler_params=pltpu.CompilerParams(dimension_semantics=("arbitrary",),
                                             vmem_limit_bytes=VMEM_LIMIT_BYTES),
        name=name,
    )(x, p, g_ple.reshape(1, -1), w_pg, w_pe, g_final.reshape(1, -1))


def kernel(x_prompt, x_sample, p_prompt, p_sample, cache_conv, g_mix, w_in, b_in, g_v, b_v, w_s, b_s, w_dw, b_dw, g_c, b_c, g_oa, g_ob, w_out, g_ffn, w1, w2, g_ple, w_pg, w_pe, g_final):
    depth, batch, seq, _ = p_prompt.shape
    _, dec_batch, dec_seq, _ = p_sample.shape
    assert depth == 1 and batch == 1
    assert seq % MIX_ROWS == 0 and seq % FFN_ROWS == 0 and seq % PLE_ROWS == 0
    assert dec_batch * dec_seq == MLP_CHUNK and dec_seq % SUBLANES == 0 and dec_seq <= CHUNK

    lw = dict(g_mix=g_mix[0], w_in=w_in[0].astype(bf16), b_in=b_in[0], g_v=g_v[0], b_v=b_v[0],
              w_dw=w_dw[0], b_dw=b_dw[0], g_c=g_c[0], b_c=b_c[0], g_oa=g_oa[0], g_ob=g_ob[0],
              w_out=w_out[0].astype(bf16))
    w1b, w2b = w1[0].astype(bf16), w2[0].astype(bf16)
    w_pgb, w_peb = w_pg[0].astype(bf16), w_pe[0].astype(bf16)
    ws, bs = w_s[0], b_s[0]

    bs_prompt = jnp.repeat(bs.T, HEAD_DIM_A, axis=1)
    hist0 = jnp.zeros((1, HIST, D_B), f32)
    xp, hist_p = _mix_call(x_prompt.reshape(seq, D_MODEL), hist0, ws, bs_prompt, lw,
                           rows=MIX_ROWS, streams=1, mask_ws=True, emit_v=False)
    xp = _ffn_call(xp, g_ffn[0], w1b, w2b, rows=FFN_ROWS, name="ffn_prompt")
    yp = _ple_call(xp, p_prompt.reshape(seq, D_PLE), g_ple[0], w_pgb, w_peb, g_final,
                   rows=PLE_ROWS, name="ple_prompt")

    qi = jnp.arange(MLP_CHUNK)
    keep = (qi[None, :] // CHUNK) <= (qi[:, None] // CHUNK)
    ws_head = jnp.where(keep[None], ws, jnp.zeros_like(ws))[:, :dec_seq, :dec_seq]
    ws_sample = jnp.einsum("st,hij->hsitj", jnp.eye(dec_batch, dtype=f32), ws_head)
    ws_sample = ws_sample.reshape(N_HEADS_A, MLP_CHUNK, MLP_CHUNK)
    bs_sample = jnp.tile(jnp.repeat(bs[:, :dec_seq].T, HEAD_DIM_A, axis=1), (dec_batch, 1))
    hist_s0 = jnp.pad(cache_conv[0], ((0, 0), (HIST_PAD, 0), (0, 0)))
    rows_s = dec_batch * dec_seq
    xs, hist_s, v_s = _mix_call(x_sample.reshape(rows_s, D_MODEL), hist_s0, ws_sample, bs_sample, lw,
                                rows=rows_s, streams=dec_batch, mask_ws=False, emit_v=True)
    xs = _ffn_call(xs, g_ffn[0], w1b, w2b, rows=rows_s, name="ffn_sample")
    ys = _ple_call(xs, p_sample.reshape(rows_s, D_PLE), g_ple[0], w_pgb, w_peb, g_final,
                   rows=rows_s, name="ple_sample")

    return (yp.reshape(batch, seq, D_MODEL),
            ys.reshape(dec_batch, dec_seq, D_MODEL),
            hist_p[:, HIST_PAD:, :][None],
            hist_s[:, HIST_PAD:, :][None],
            v_s.reshape(1, dec_batch, dec_seq, D_A))
```

```python
import functools

import jax
import jax.numpy as jnp
from jax import lax
from jax.experimental import pallas as pl
from jax.experimental.pallas import tpu as pltpu

D_MODEL = 2048
D_A = 1024
D_B = 1024
N_HEADS_A = 8
HEAD_DIM_A = D_A // N_HEADS_A
MLP_CHUNK = 128
CHUNK = 64
CONV_WIDTH = 31
D_FF = 4 * D_MODEL
D_PLE = 256
EPS = 1e-6

SUBLANES = 8
LANES = 128
BF16_SUBLANES = 16
SEGMENTS = SUBLANES
HIST = 32
HIST_PAD = HIST - (CONV_WIDTH - 1)
VMEM_LIMIT_BYTES = 60 * 1024 * 1024

MIX_ROWS = 256
FFN_ROWS = 512
FFN_COLS = 1024
PLE_ROWS = 512
IN_PROJ_PIECES = 8
PIECES_BEFORE_CONV = 4

f32 = jnp.float32
bf16 = jnp.bfloat16


def _rms(x, g):
    return x * lax.rsqrt(jnp.mean(x * x, axis=-1, keepdims=True) + EPS) * g


def _ln(x, g, b):
    mu = jnp.mean(x, axis=-1, keepdims=True)
    xc = x - mu
    var = jnp.mean(xc * xc, axis=-1, keepdims=True)
    return xc * lax.rsqrt(var + EPS) * g + b


def _pitch(n):
    return ((-(-n // SUBLANES)) | 1) * SUBLANES


_SLABS = [slice(l * LANES, (l + 1) * LANES) for l in range(D_B // LANES)]


def _load_conv_history(hist_in_ref, stage_ref, *, seg, chained):
    pin = _pitch(HIST + seg)
    for r in range(1 if chained else SEGMENTS):
        for l, lanes in enumerate(_SLABS):
            stage_ref[l, r * pin:r * pin + HIST, :] = hist_in_ref[r, :, lanes]


def _stage_conv_taps(glu, stage_ref, taps_ref, *, chained):
    rows = glu.shape[0]
    seg = rows // SEGMENTS
    pin = _pitch(HIST + seg)
    for r in range(SEGMENTS):
        for l, lanes in enumerate(_SLABS):
            stage_ref[l, r * pin + HIST:r * pin + HIST + seg, :] = glu[r * seg:(r + 1) * seg, lanes]
            if chained and r > 0:
                stage_ref[l, r * pin:r * pin + HIST, :] = glu[r * seg - HIST:r * seg, lanes]
    for q in range(seg + CONV_WIDTH - 1):
        for l in range(len(_SLABS)):
            taps_ref[l, q * SEGMENTS:(q + 1) * SEGMENTS, :] = (
                stage_ref[l, pl.ds(HIST_PAD + q, SEGMENTS, stride=pin), :])
    if chained:
        for l, lanes in enumerate(_SLABS):
            stage_ref[l, 0:HIST, :] = glu[rows - HIST:, lanes]


def _conv_slab(l, rows, w_dw_ref, b_dw_ref, taps_ref, unperm_ref, after):
    seg = rows // SEGMENTS
    pout = _pitch(seg)
    lanes = _SLABS[l]
    c = jnp.broadcast_to(after(b_dw_ref[:, lanes]), (rows, LANES))
    for k in range(CONV_WIDTH):
        c = c + after(w_dw_ref[k:k + 1, lanes]) * taps_ref[l, k * SEGMENTS:k * SEGMENTS + rows, :]
    for b in range(seg):
        unperm_ref[l, pl.ds(b, SEGMENTS, stride=pout), :] = c[b * SEGMENTS:(b + 1) * SEGMENTS, :]
    return jnp.concatenate([unperm_ref[l, r * pout:r * pout + seg, :] for r in range(SEGMENTS)], axis=0)


def _new_conv_history(glu, hist_in_ref, *, chained):
    rows = glu.shape[0]
    seg = rows // SEGMENTS
    if chained:
        return glu[rows - HIST:, :][None]
    if seg >= HIST:
        return jnp.stack([glu[(r + 1) * seg - HIST:(r + 1) * seg, :] for r in range(SEGMENTS)])
    return jnp.stack([jnp.concatenate([hist_in_ref[r, seg:, :], glu[r * seg:(r + 1) * seg, :]], axis=0)
                      for r in range(SEGMENTS)])


def _mix_body(x_ref, x_prev_ref, z_ref, z_next_ref, hn_ref, hist_in_ref, g_mix_ref, w_in_ref, b_in_ref,
              g_v_ref, b_v_ref, ws_ref, bs_ref, w_dw_ref, b_dw_ref, g_c_ref, b_c_ref, g_oa_ref, g_ob_ref,
              w_out_ref, o_ref, hist_out_ref, v_ref, stage_ref, taps_ref, unperm_ref, *, chained, mask_ws):
    pipelined = z_ref is not None
    rows = x_ref.shape[0]
    cols = (2 * D_A + 2 * D_B) // IN_PROJ_PIECES

    def _zero_of(val):
        last = val[val.shape[0] - SUBLANES:, val.shape[1] - LANES:]
        return jnp.where(jnp.isfinite(last), last, 0.0) * 0.0

    def in_proj_piece(j, prev_stage=None):
        if not pipelined:
            return None
        if prev_stage is None:
            lhs = hn_ref[...]
        else:
            offset = _zero_of(prev_stage).astype(jnp.int32)[0, 0]
            lhs = hn_ref[pl.ds(pl.multiple_of(offset, BF16_SUBLANES), rows), :]
        zj = jnp.dot(lhs, pltpu.bitcast(w_in_ref[:, j * cols:(j + 1) * cols], bf16),
                     preferred_element_type=f32)
        z_next_ref[:, j * cols:(j + 1) * cols] = zj
        return _zero_of(zj)[0:1, :]

    def after(tie, row):
        return row if tie is None else row + jnp.tile(tie, (1, row.shape[1] // LANES))

    if pipelined:
        hn_ref[...] = _rms(x_ref[...], g_mix_ref[...]).astype(bf16)
        x = x_prev_ref[...]
        zcols = lambda lo, hi, tie=None: z_ref[:, lo:hi] + after(tie, b_in_ref[:, lo:hi])
    else:
        x = x_ref[...]
        z = jnp.dot(_rms(x, g_mix_ref[...]).astype(bf16), pltpu.bitcast(w_in_ref[...], bf16),
                    preferred_element_type=f32)
        zcols = lambda lo, hi, tie=None: z[:, lo:hi] + b_in_ref[:, lo:hi]

    tie = in_proj_piece(0)
    u = jax.nn.gelu(zcols(0, D_A))
    tie, prev_tie = in_proj_piece(1, u), tie
    v = _ln(jax.nn.gelu(zcols(D_A, 2 * D_A, prev_tie)), g_v_ref[...], b_v_ref[...])
    if v_ref is not None:
        v_ref[...] = v
    ws = ws_ref[...]
    if mask_ws:
        qi = lax.broadcasted_iota(jnp.int32, (MLP_CHUNK, MLP_CHUNK), 0)
        ki = lax.broadcasted_iota(jnp.int32, (MLP_CHUNK, MLP_CHUNK), 1)
        keep = (ki // CHUNK) <= (qi // CHUNK)
        ws = jnp.where(keep[None], ws, jnp.zeros_like(ws))
    ws = ws.astype(bf16)
    vb = v.astype(bf16)
    mixed_chunks = []
    for c in range(rows // MLP_CHUNK):
        r0 = c * MLP_CHUNK
        heads = [
            jnp.dot(ws[hd], vb[r0:r0 + MLP_CHUNK, hd * HEAD_DIM_A:(hd + 1) * HEAD_DIM_A],
                    preferred_element_type=f32)
            for hd in range(N_HEADS_A)
        ]
        mixed_chunks.append(jnp.concatenate(heads, axis=1) + bs_ref[...])
    mixed = mixed_chunks[0] if len(mixed_chunks) == 1 else jnp.concatenate(mixed_chunks, axis=0)
    tie, prev_tie = in_proj_piece(2, v), tie
    y_a = _rms(u * mixed, after(prev_tie, g_oa_ref[...]))
    o = jnp.dot(y_a.astype(bf16), pltpu.bitcast(w_out_ref[0:D_A // 2, :], bf16), preferred_element_type=f32)

    tie, prev_tie = in_proj_piece(3, y_a), tie
    glu = (zcols(2 * D_A, 2 * D_A + D_B, prev_tie)
           * jax.nn.sigmoid(zcols(2 * D_A + D_B, 2 * D_A + 2 * D_B, prev_tie)))
    hist_out_ref[...] = _new_conv_history(glu, hist_in_ref, chained=chained)
    _stage_conv_taps(glu, stage_ref, taps_ref, chained=chained)
    conv_slabs = []
    slabs_per_piece = len(_SLABS) // (IN_PROJ_PIECES - PIECES_BEFORE_CONV)
    stage_out = glu
    for l in range(len(_SLABS)):
        if l % slabs_per_piece == 0:
            tie, prev_tie = in_proj_piece(PIECES_BEFORE_CONV + l // slabs_per_piece, stage_out), tie
        conv_slabs.append(_conv_slab(l, rows, w_dw_ref, b_dw_ref, taps_ref, unperm_ref,
                                     functools.partial(after, prev_tie)))
        stage_out = conv_slabs[-1]
    c = jnp.concatenate(conv_slabs, axis=1)
    y_b = _rms(jax.nn.silu(_ln(c, g_c_ref[...], after(tie, b_c_ref[...]))), g_ob_ref[...]).astype(bf16)
    o = o + jnp.dot(y_b, pltpu.bitcast(w_out_ref[D_A // 2:, :], bf16), preferred_element_type=f32)
    o_ref[...] = x + o


def _mix_kernel(*refs, chained, mask_ws, emit_v, pipelined):
    refs = list(refs)
    x_ref = refs.pop(0)
    x_prev_ref = refs.pop(0) if pipelined else None
    params = refs[:15]
    hist_in_ref = params[0]
    outs = refs[15:]
    o_ref, hist_out_ref = outs[:2]
    v_ref = outs[2] if emit_v else None
    stage_ref, taps_ref, unperm_ref = outs[-3:]
    rows = x_ref.shape[0]
    step = pl.program_id(0)
    body = functools.partial(_mix_body, chained=chained, mask_ws=mask_ws)
    tail = (o_ref, hist_out_ref, v_ref, stage_ref, taps_ref, unperm_ref)

    if not pipelined:
        _load_conv_history(hist_in_ref, stage_ref, seg=rows // SEGMENTS, chained=chained)
        body(x_ref, None, None, None, None, *params, *tail)
        return

    z0_ref, z1_ref, hn_ref = outs[-6:-3]

    @pl.when(step == 0)
    def _():
        z1_ref[...] = jnp.zeros_like(z1_ref)
        stage_ref[...] = jnp.zeros_like(stage_ref)

    @pl.when(step == 1)
    def _():
        _load_conv_history(hist_in_ref, stage_ref, seg=rows // SEGMENTS, chained=chained)

    @pl.when(step % 2 == 0)
    def _():
        body(x_ref, x_prev_ref, z1_ref, z0_ref, hn_ref, *params, *tail)

    @pl.when(step % 2 == 1)
    def _():
        body(x_ref, x_prev_ref, z0_ref, z1_ref, hn_ref, *params, *tail)


def _pack_rows(w):
    k, n = w.shape
    pairs = jnp.swapaxes(w.astype(bf16).reshape(k // 2, 2, n), 1, 2)
    return lax.bitcast_convert_type(pairs, jnp.uint32)


def _const_spec(shape):
    zeros = (0,) * len(shape)
    return pl.BlockSpec(shape, lambda *_: zeros, pipeline_mode=pl.Buffered(1))


def _mix_call(x, hist_in, ws, bs, lw, *, rows, chained, mask_ws, emit_v, pipelined):
    n = x.shape[0]
    tiles = n // rows
    streams = hist_in.shape[0]
    seg = rows // SEGMENTS
    assert streams == (1 if chained else SEGMENTS) and (seg >= HIST or not chained)
    assert pipelined or tiles == 1
    n_slabs = D_B // LANES
    vec = lambda a: a.reshape(1, -1)
    params = [hist_in, vec(lw["g_mix"]), lw["w_in"], vec(lw["b_in"]), vec(lw["g_v"]), vec(lw["b_v"]),
              ws, bs, lw["w_dw"], vec(lw["b_dw"]), vec(lw["g_c"]), vec(lw["b_c"]),
              vec(lw["g_oa"]), vec(lw["g_ob"]), lw["w_out"]]
    scratch = [pltpu.VMEM((n_slabs, SEGMENTS * _pitch(HIST + seg), LANES), f32),
               pltpu.VMEM((n_slabs, (seg + CONV_WIDTH - 1) * SEGMENTS, LANES), f32),
               pltpu.VMEM((n_slabs, SEGMENTS * _pitch(seg), LANES), f32)]
    if pipelined:
        grid = (tiles + 1,)
        rows_of = lambda cols, idx: pl.BlockSpec((rows, cols), lambda s: (idx(s), 0))
        cur = lambda s: jnp.minimum(s, tiles - 1)
        prev = lambda s: jnp.maximum(s - 1, 0)
        operands = [x, x] + params
        in_specs = [rows_of(D_MODEL, cur), rows_of(D_MODEL, prev)]
        out_rows = lambda cols: rows_of(cols, prev)
        z_buf = pltpu.VMEM((rows, 2 * D_A + 2 * D_B), f32)
        scratch = [z_buf, z_buf, pltpu.VMEM((rows, D_MODEL), bf16)] + scratch
    else:
        grid = (tiles,)
        operands = [x] + params
        out_rows = lambda cols: pl.BlockSpec((rows, cols), lambda s: (s, 0))
        in_specs = [out_rows(D_MODEL)]
    in_specs += [_const_spec(a.shape) for a in params]
    out_shape = [jax.ShapeDtypeStruct((n, D_MODEL), f32),
                 jax.ShapeDtypeStruct((streams, HIST, D_B), f32)]
    out_specs = [out_rows(D_MODEL), _const_spec((streams, HIST, D_B))]
    if emit_v:
        out_shape.append(jax.ShapeDtypeStruct((n, D_A), f32))
        out_specs.append(out_rows(D_A))
    return pl.pallas_call(
        functools.partial(_mix_kernel, chained=chained, mask_ws=mask_ws, emit_v=emit_v, pipelined=pipelined),
        grid=grid,
        in_specs=in_specs,
        out_specs=out_specs,
        out_shape=out_shape,
        scratch_shapes=scratch,
        compiler_params=pltpu.CompilerParams(dimension_semantics=("arbitrary",),
                                             vmem_limit_bytes=VMEM_LIMIT_BYTES),
        name="mix_prompt" if pipelined else "mix_sample",
    )(*operands)


def _ffn_kernel(x_ref, g_ref, w1_ref, w2_ref, o_ref, hn_ref, acc_ref):
    f = pl.program_id(1)

    @pl.when(f == 0)
    def _():
        hn_ref[...] = _rms(x_ref[...], g_ref[...]).astype(bf16)
        acc_ref[...] = jnp.zeros_like(acc_ref)

    h = jnp.dot(hn_ref[...], w1_ref[...], preferred_element_type=f32)
    h = jnp.square(jnp.maximum(h, 0.0)).astype(bf16)
    acc_ref[...] += jnp.dot(h, w2_ref[...], preferred_element_type=f32)

    @pl.when(f == pl.num_programs(1) - 1)
    def _():
        o_ref[...] = x_ref[...] + acc_ref[...]


def _ffn_call(x, g_ffn, w1, w2, *, rows, name):
    n = x.shape[0]
    return pl.pallas_call(
        _ffn_kernel,
        grid=(n // rows, D_FF // FFN_COLS),
        in_specs=[pl.BlockSpec((rows, D_MODEL), lambda i, f: (i, 0)),
                  _const_spec((1, D_MODEL)),
                  pl.BlockSpec((D_MODEL, FFN_COLS), lambda i, f: (0, f)),
                  pl.BlockSpec((FFN_COLS, D_MODEL), lambda i, f: (f, 0))],
        out_specs=pl.BlockSpec((rows, D_MODEL), lambda i, f: (i, 0)),
        out_shape=jax.ShapeDtypeStruct((n, D_MODEL), f32),
        scratch_shapes=[pltpu.VMEM((rows, D_MODEL), bf16), pltpu.VMEM((rows, D_MODEL), f32)],
        compiler_params=pltpu.CompilerParams(dimension_semantics=("arbitrary", "arbitrary"),
                                             vmem_limit_bytes=VMEM_LIMIT_BYTES),
        name=name,
    )(x, g_ffn.reshape(1, -1), w1, w2)


def _ple_kernel(x_ref, p_ref, g_ple_ref, w_pg_ref, w_pe_ref, g_final_ref, o_ref):
    x = x_ref[...]
    gate = jax.nn.sigmoid(jnp.dot(_rms(x, g_ple_ref[...]).astype(bf16), w_pg_ref[...],
                                  preferred_element_type=f32))
    emb = jnp.dot(p_ref[...].astype(bf16), w_pe_ref[...], preferred_element_type=f32)
    o_ref[...] = _rms(x + emb * gate, g_final_ref[...])


def _ple_call(x, p, g_ple, w_pg, w_pe, g_final, *, rows, name):
    n = x.shape[0]
    return pl.pallas_call(
        _ple_kernel,
        grid=(n // rows,),
        in_specs=[pl.BlockSpec((rows, D_MODEL), lambda i: (i, 0)),
                  pl.BlockSpec((rows, D_PLE), lambda i: (i, 0)),
                  _const_spec((1, D_MODEL)),
                  _const_spec((D_MODEL, D_MODEL)),
                  _const_spec((D_PLE, D_MODEL)),
                  _const_spec((1, D_MODEL))],
        out_specs=pl.BlockSpec((rows, D_MODEL), lambda i: (i, 0)),
        out_shape=jax.ShapeDtypeStruct((n, D_MODEL), f32),
        compiler_params=pltpu.CompilerParams(dimension_semantics=("arbitrary",),
                                             vmem_limit_bytes=VMEM_LIMIT_BYTES),
        name=name,
    )(x, p, g_ple.reshape(1, -1), w_pg, w_pe, g_final.reshape(1, -1))


def kernel(x_prompt, x_sample, p_prompt, p_sample, cache_conv, g_mix, w_in, b_in, g_v, b_v, w_s, b_s, w_dw, b_dw, g_c, b_c, g_oa, g_ob, w_out, g_ffn, w1, w2, g_ple, w_pg, w_pe, g_final):
    depth, batch, seq, _ = p_prompt.shape
    _, dec_batch, dec_seq, _ = p_sample.shape
    assert depth == 1 and batch == 1
    assert seq % MIX_ROWS == 0 and seq % FFN_ROWS == 0 and seq % PLE_ROWS == 0
    assert dec_batch == SEGMENTS and dec_batch * dec_seq == MLP_CHUNK and dec_seq % SUBLANES == 0
    assert dec_seq <= CHUNK

    lw = dict(g_mix=g_mix[0], w_in=_pack_rows(w_in[0]), b_in=b_in[0], g_v=g_v[0], b_v=b_v[0],
              w_dw=w_dw[0], b_dw=b_dw[0], g_c=g_c[0], b_c=b_c[0], g_oa=g_oa[0], g_ob=g_ob[0],
              w_out=_pack_rows(w_out[0]))
    w1b, w2b = w1[0].astype(bf16), w2[0].astype(bf16)
    w_pgb, w_peb = w_pg[0].astype(bf16), w_pe[0].astype(bf16)
    ws, bs = w_s[0], b_s[0]

    bs_prompt = jnp.repeat(bs.T, HEAD_DIM_A, axis=1)
    hist0 = jnp.zeros((1, HIST, D_B), f32)
    xp, hist_p = _mix_call(x_prompt.reshape(seq, D_MODEL), hist0, ws, bs_prompt, lw,
                           rows=MIX_ROWS, chained=True, mask_ws=True, emit_v=False, pipelined=True)
    xp = _ffn_call(xp, g_ffn[0], w1b, w2b, rows=FFN_ROWS, name="ffn_prompt")
    yp = _ple_call(xp, p_prompt.reshape(seq, D_PLE), g_ple[0], w_pgb, w_peb, g_final,
                   rows=PLE_ROWS, name="ple_prompt")

    qi = jnp.arange(MLP_CHUNK)
    keep = (qi[None, :] // CHUNK) <= (qi[:, None] // CHUNK)
    ws_head = jnp.where(keep[None], ws, jnp.zeros_like(ws))[:, :dec_seq, :dec_seq]
    ws_sample = jnp.einsum("st,hij->hsitj", jnp.eye(dec_batch, dtype=f32), ws_head)
    ws_sample = ws_sample.reshape(N_HEADS_A, MLP_CHUNK, MLP_CHUNK)
    bs_sample = jnp.tile(jnp.repeat(bs[:, :dec_seq].T, HEAD_DIM_A, axis=1), (dec_batch, 1))
    hist_s0 = jnp.pad(cache_conv[0], ((0, 0), (HIST_PAD, 0), (0, 0)))
    rows_s = dec_batch * dec_seq
    xs, hist_s, v_s = _mix_call(x_sample.reshape(rows_s, D_MODEL), hist_s0, ws_sample, bs_sample, lw,
                                rows=rows_s, chained=False, mask_ws=False, emit_v=True, pipelined=False)
    xs = _ffn_call(xs, g_ffn[0], w1b, w2b, rows=rows_s, name="ffn_sample")
    ys = _ple_call(xs, p_sample.reshape(rows_s, D_PLE), g_ple[0], w_pgb, w_peb, g_final,
                   rows=rows_s, name="ple_sample")

    return (yp.reshape(batch, seq, D_MODEL),
            ys.reshape(dec_batch, dec_seq, D_MODEL),
            hist_p[:, HIST_PAD:, :][None],
            hist_s[:, HIST_PAD:, :][None],
            v_s.reshape(1, dec_batch, dec_seq, D_A))
```

```python
import functools

import jax
import jax.numpy as jnp
from jax import lax
from jax.experimental import pallas as pl
from jax.experimental.pallas import tpu as pltpu

D_MODEL = 2048
D_A = 1024
D_B = 1024
N_HEADS_A = 8
HEAD_DIM_A = D_A // N_HEADS_A
MLP_CHUNK = 128
CHUNK = 64
CONV_WIDTH = 31
D_FF = 4 * D_MODEL
D_PLE = 256
EPS = 1e-6

SUBLANES = 8
LANES = 128
BF16_SUBLANES = 16
SEGMENTS = SUBLANES
HIST = 32
HIST_PAD = HIST - (CONV_WIDTH - 1)
VMEM_LIMIT_BYTES = 60 * 1024 * 1024

MIX_ROWS = 256
FFN_ROWS = 512
FFN_COLS = 1024
PLE_ROWS = 512
PACK_BLOCK_BYTES = 4 * 1024 * 1024
IN_PROJ_PIECES = 8
PIECES_BEFORE_CONV = 4

f32 = jnp.float32
bf16 = jnp.bfloat16


def _rms(x, g):
    return x * lax.rsqrt(jnp.mean(x * x, axis=-1, keepdims=True) + EPS) * g


def _ln(x, g, b):
    mu = jnp.mean(x, axis=-1, keepdims=True)
    xc = x - mu
    var = jnp.mean(xc * xc, axis=-1, keepdims=True)
    return xc * lax.rsqrt(var + EPS) * g + b


def _pitch(n):
    return ((-(-n // SUBLANES)) | 1) * SUBLANES


_SLABS = [slice(l * LANES, (l + 1) * LANES) for l in range(D_B // LANES)]


def _load_conv_history(hist_in_ref, stage_ref, *, seg, chained):
    pin = _pitch(HIST + seg)
    for r in range(1 if chained else SEGMENTS):
        for l, lanes in enumerate(_SLABS):
            stage_ref[l, r * pin:r * pin + HIST, :] = hist_in_ref[r, :, lanes]


def _stage_conv_taps(glu, stage_ref, taps_ref, *, chained):
    rows = glu.shape[0]
    seg = rows // SEGMENTS
    pin = _pitch(HIST + seg)
    for r in range(SEGMENTS):
        for l, lanes in enumerate(_SLABS):
            stage_ref[l, r * pin + HIST:r * pin + HIST + seg, :] = glu[r * seg:(r + 1) * seg, lanes]
            if chained and r > 0:
                stage_ref[l, r * pin:r * pin + HIST, :] = glu[r * seg - HIST:r * seg, lanes]
    for q in range(seg + CONV_WIDTH - 1):
        for l in range(len(_SLABS)):
            taps_ref[l, q * SEGMENTS:(q + 1) * SEGMENTS, :] = (
                stage_ref[l, pl.ds(HIST_PAD + q, SEGMENTS, stride=pin), :])
    if chained:
        for l, lanes in enumerate(_SLABS):
            stage_ref[l, 0:HIST, :] = glu[rows - HIST:, lanes]


def _conv_slab(l, rows, w_dw_ref, b_dw_ref, taps_ref, unperm_ref, after):
    seg = rows // SEGMENTS
    pout = _pitch(seg)
    lanes = _SLABS[l]
    c = jnp.broadcast_to(after(b_dw_ref[:, lanes]), (rows, LANES))
    for k in range(CONV_WIDTH):
        c = c + after(w_dw_ref[k:k + 1, lanes]) * taps_ref[l, k * SEGMENTS:k * SEGMENTS + rows, :]
    for b in range(seg):
        unperm_ref[l, pl.ds(b, SEGMENTS, stride=pout), :] = c[b * SEGMENTS:(b + 1) * SEGMENTS, :]
    return jnp.concatenate([unperm_ref[l, r * pout:r * pout + seg, :] for r in range(SEGMENTS)], axis=0)


def _new_conv_history(glu, hist_in_ref, *, chained):
    rows = glu.shape[0]
    seg = rows // SEGMENTS
    if chained:
        return glu[rows - HIST:, :][None]
    if seg >= HIST:
        return jnp.stack([glu[(r + 1) * seg - HIST:(r + 1) * seg, :] for r in range(SEGMENTS)])
    return jnp.stack([jnp.concatenate([hist_in_ref[r, seg:, :], glu[r * seg:(r + 1) * seg, :]], axis=0)
                      for r in range(SEGMENTS)])


def _mix_body(x_ref, x_prev_ref, z_ref, z_next_ref, hn_ref, hist_in_ref, g_mix_ref, w_in_ref, b_in_ref,
              g_v_ref, b_v_ref, ws_ref, bs_ref, w_dw_ref, b_dw_ref, g_c_ref, b_c_ref, g_oa_ref, g_ob_ref,
              w_out_ref, o_ref, hist_out_ref, v_ref, stage_ref, taps_ref, unperm_ref, *, chained, mask_ws):
    pipelined = z_ref is not None
    rows = x_ref.shape[0]
    cols = (2 * D_A + 2 * D_B) // IN_PROJ_PIECES

    def _zero_of(val):
        last = val[val.shape[0] - SUBLANES:, val.shape[1] - LANES:]
        return jnp.where(jnp.isfinite(last), last, 0.0) * 0.0

    def in_proj_piece(j, prev_stage=None):
        if not pipelined:
            return None
        if prev_stage is None:
            lhs = hn_ref[...]
        else:
            offset = _zero_of(prev_stage).astype(jnp.int32)[0, 0]
            lhs = hn_ref[pl.ds(pl.multiple_of(offset, BF16_SUBLANES), rows), :]
        zj = jnp.dot(lhs, pltpu.bitcast(w_in_ref[:, j * cols:(j + 1) * cols], bf16),
                     preferred_element_type=f32)
        z_next_ref[:, j * cols:(j + 1) * cols] = zj
        return _zero_of(zj)[0:1, :]

    def after(tie, row):
        return row if tie is None else row + jnp.tile(tie, (1, row.shape[1] // LANES))

    if pipelined:
        hn_ref[...] = _rms(x_ref[...], g_mix_ref[...]).astype(bf16)
        x = x_prev_ref[...]
        zcols = lambda lo, hi, tie=None: z_ref[:, lo:hi] + after(tie, b_in_ref[:, lo:hi])
    else:
        x = x_ref[...]
        z = jnp.dot(_rms(x, g_mix_ref[...]).astype(bf16), pltpu.bitcast(w_in_ref[...], bf16),
                    preferred_element_type=f32)
        zcols = lambda lo, hi, tie=None: z[:, lo:hi] + b_in_ref[:, lo:hi]

    tie = in_proj_piece(0)
    u = jax.nn.gelu(zcols(0, D_A))
    tie, prev_tie = in_proj_piece(1, u), tie
    v = _ln(jax.nn.gelu(zcols(D_A, 2 * D_A, prev_tie)), g_v_ref[...], b_v_ref[...])
    if v_ref is not None:
        v_ref[...] = v
    ws = ws_ref[...]
    if mask_ws:
        qi = lax.broadcasted_iota(jnp.int32, (MLP_CHUNK, MLP_CHUNK), 0)
        ki = lax.broadcasted_iota(jnp.int32, (MLP_CHUNK, MLP_CHUNK), 1)
        keep = (ki // CHUNK) <= (qi // CHUNK)
        ws = jnp.where(keep[None], ws, jnp.zeros_like(ws))
    ws = ws.astype(bf16)
    vb = v.astype(bf16)
    mixed_chunks = []
    for c in range(rows // MLP_CHUNK):
        r0 = c * MLP_CHUNK
        heads = [
            jnp.dot(ws[hd], vb[r0:r0 + MLP_CHUNK, hd * HEAD_DIM_A:(hd + 1) * HEAD_DIM_A],
                    preferred_element_type=f32)
            for hd in range(N_HEADS_A)
        ]
        mixed_chunks.append(jnp.concatenate(heads, axis=1) + bs_ref[...])
    mixed = mixed_chunks[0] if len(mixed_chunks) == 1 else jnp.concatenate(mixed_chunks, axis=0)
    tie, prev_tie = in_proj_piece(2, v), tie
    y_a = _rms(u * mixed, after(prev_tie, g_oa_ref[...]))
    o = jnp.dot(y_a.astype(bf16), pltpu.bitcast(w_out_ref[0:D_A // 2, :], bf16), preferred_element_type=f32)

    tie, prev_tie = in_proj_piece(3, y_a), tie
    glu = (zcols(2 * D_A, 2 * D_A + D_B, prev_tie)
           * jax.nn.sigmoid(zcols(2 * D_A + D_B, 2 * D_A + 2 * D_B, prev_tie)))
    hist_out_ref[...] = _new_conv_history(glu, hist_in_ref, chained=chained)
    _stage_conv_taps(glu, stage_ref, taps_ref, chained=chained)
    conv_slabs = []
    slabs_per_piece = len(_SLABS) // (IN_PROJ_PIECES - PIECES_BEFORE_CONV)
    stage_out = glu
    for l in range(len(_SLABS)):
        if l % slabs_per_piece == 0:
            tie, prev_tie = in_proj_piece(PIECES_BEFORE_CONV + l // slabs_per_piece, stage_out), tie
        conv_slabs.append(_conv_slab(l, rows, w_dw_ref, b_dw_ref, taps_ref, unperm_ref,
                                     functools.partial(after, prev_tie)))
        stage_out = conv_slabs[-1]
    c = jnp.concatenate(conv_slabs, axis=1)
    y_b = _rms(jax.nn.silu(_ln(c, g_c_ref[...], after(tie, b_c_ref[...]))), g_ob_ref[...]).astype(bf16)
    o = o + jnp.dot(y_b, pltpu.bitcast(w_out_ref[D_A // 2:, :], bf16), preferred_element_type=f32)
    o_ref[...] = x + o


def _mix_kernel(*refs, chained, mask_ws, emit_v, pipelined):
    refs = list(refs)
    x_ref = refs.pop(0)
    x_prev_ref = refs.pop(0) if pipelined else None
    params = refs[:15]
    hist_in_ref = params[0]
    outs = refs[15:]
    o_ref, hist_out_ref = outs[:2]
    v_ref = outs[2] if emit_v else None
    stage_ref, taps_ref, unperm_ref = outs[-3:]
    rows = x_ref.shape[0]
    step = pl.program_id(0)
    body = functools.partial(_mix_body, chained=chained, mask_ws=mask_ws)
    tail = (o_ref, hist_out_ref, v_ref, stage_ref, taps_ref, unperm_ref)

    if not pipelined:
        _load_conv_history(hist_in_ref, stage_ref, seg=rows // SEGMENTS, chained=chained)
        body(x_ref, None, None, None, None, *params, *tail)
        return

    z0_ref, z1_ref, hn_ref = outs[-6:-3]

    @pl.when(step == 0)
    def _():
        z1_ref[...] = jnp.zeros_like(z1_ref)
        stage_ref[...] = jnp.zeros_like(stage_ref)

    @pl.when(step == 1)
    def _():
        _load_conv_history(hist_in_ref, stage_ref, seg=rows // SEGMENTS, chained=chained)

    @pl.when(step % 2 == 0)
    def _():
        body(x_ref, x_prev_ref, z1_ref, z0_ref, hn_ref, *params, *tail)

    @pl.when(step % 2 == 1)
    def _():
        body(x_ref, x_prev_ref, z0_ref, z1_ref, hn_ref, *params, *tail)


def _pack_kernel(w_ref, o_ref):
    o_ref[...] = pltpu.bitcast(w_ref[...].astype(bf16), jnp.uint32)


def _pack_rows(w, name):
    k, n = w.shape
    rows = max(BF16_SUBLANES, min(k, PACK_BLOCK_BYTES // (4 * n)))
    assert k % rows == 0 and rows % BF16_SUBLANES == 0
    return pl.pallas_call(
        _pack_kernel,
        grid=(k // rows,),
        in_specs=[pl.BlockSpec((rows, n), lambda i: (i, 0))],
        out_specs=pl.BlockSpec((rows // 2, n), lambda i: (i, 0)),
        out_shape=jax.ShapeDtypeStruct((k // 2, n), jnp.uint32),
        compiler_params=pltpu.CompilerParams(dimension_semantics=("arbitrary",),
                                             vmem_limit_bytes=VMEM_LIMIT_BYTES),
        name=name,
    )(w)


def _const_spec(shape):
    zeros = (0,) * len(shape)
    return pl.BlockSpec(shape, lambda *_: zeros, pipeline_mode=pl.Buffered(1))


def _mix_call(x, hist_in, ws, bs, lw, *, rows, chained, mask_ws, emit_v, pipelined):
    n = x.shape[0]
    tiles = n // rows
    streams = hist_in.shape[0]
    seg = rows // SEGMENTS
    assert streams == (1 if chained else SEGMENTS) and (seg >= HIST or not chained)
    assert pipelined or tiles == 1
    n_slabs = D_B // LANES
    vec = lambda a: a.reshape(1, -1)
    params = [hist_in, vec(lw["g_mix"]), lw["w_in"], vec(lw["b_in"]), vec(lw["g_v"]), vec(lw["b_v"]),
              ws, bs, lw["w_dw"], vec(lw["b_dw"]), vec(lw["g_c"]), vec(lw["b_c"]),
              vec(lw["g_oa"]), vec(lw["g_ob"]), lw["w_out"]]
    scratch = [pltpu.VMEM((n_slabs, SEGMENTS * _pitch(HIST + seg), LANES), f32),
               pltpu.VMEM((n_slabs, (seg + CONV_WIDTH - 1) * SEGMENTS, LANES), f32),
               pltpu.VMEM((n_slabs, SEGMENTS * _pitch(seg), LANES), f32)]
    if pipelined:
        grid = (tiles + 1,)
        rows_of = lambda cols, idx: pl.BlockSpec((rows, cols), lambda s: (idx(s), 0))
        cur = lambda s: jnp.minimum(s, tiles - 1)
        prev = lambda s: jnp.maximum(s - 1, 0)
        operands = [x, x] + params
        in_specs = [rows_of(D_MODEL, cur), rows_of(D_MODEL, prev)]
        out_rows = lambda cols: rows_of(cols, prev)
        z_buf = pltpu.VMEM((rows, 2 * D_A + 2 * D_B), f32)
        scratch = [z_buf, z_buf, pltpu.VMEM((rows, D_MODEL), bf16)] + scratch
    else:
        grid = (tiles,)
        operands = [x] + params
        out_rows = lambda cols: pl.BlockSpec((rows, cols), lambda s: (s, 0))
        in_specs = [out_rows(D_MODEL)]
    in_specs += [_const_spec(a.shape) for a in params]
    out_shape = [jax.ShapeDtypeStruct((n, D_MODEL), f32),
                 jax.ShapeDtypeStruct((streams, HIST, D_B), f32)]
    out_specs = [out_rows(D_MODEL), _const_spec((streams, HIST, D_B))]
    if emit_v:
        out_shape.append(jax.ShapeDtypeStruct((n, D_A), f32))
        out_specs.append(out_rows(D_A))
    return pl.pallas_call(
        functools.partial(_mix_kernel, chained=chained, mask_ws=mask_ws, emit_v=emit_v, pipelined=pipelined),
        grid=grid,
        in_specs=in_specs,
        out_specs=out_specs,
        out_shape=out_shape,
        scratch_shapes=scratch,
        compiler_params=pltpu.CompilerParams(dimension_semantics=("arbitrary",),
                                             vmem_limit_bytes=VMEM_LIMIT_BYTES),
        name="mix_prompt" if pipelined else "mix_sample",
    )(*operands)


def _ffn_kernel(x_ref, g_ref, w1_ref, w2_ref, o_ref, hn_ref, acc_ref):
    f = pl.program_id(1)

    @pl.when(f == 0)
    def _():
        hn_ref[...] = _rms(x_ref[...], g_ref[...]).astype(bf16)
        acc_ref[...] = jnp.zeros_like(acc_ref)

    h = jnp.dot(hn_ref[...], pltpu.bitcast(w1_ref[...], bf16), preferred_element_type=f32)
    h = jnp.square(jnp.maximum(h, 0.0)).astype(bf16)
    acc_ref[...] += jnp.dot(h, pltpu.bitcast(w2_ref[...], bf16), preferred_element_type=f32)

    @pl.when(f == pl.num_programs(1) - 1)
    def _():
        o_ref[...] = x_ref[...] + acc_ref[...]


def _ffn_call(x, g_ffn, w1, w2, *, rows, name):
    n = x.shape[0]
    return pl.pallas_call(
        _ffn_kernel,
        grid=(n // rows, D_FF // FFN_COLS),
        in_specs=[pl.BlockSpec((rows, D_MODEL), lambda i, f: (i, 0)),
                  _const_spec((1, D_MODEL)),
                  pl.BlockSpec((D_MODEL // 2, FFN_COLS), lambda i, f: (0, f)),
                  pl.BlockSpec((FFN_COLS // 2, D_MODEL), lambda i, f: (f, 0))],
        out_specs=pl.BlockSpec((rows, D_MODEL), lambda i, f: (i, 0)),
        out_shape=jax.ShapeDtypeStruct((n, D_MODEL), f32),
        scratch_shapes=[pltpu.VMEM((rows, D_MODEL), bf16), pltpu.VMEM((rows, D_MODEL), f32)],
        compiler_params=pltpu.CompilerParams(dimension_semantics=("arbitrary", "arbitrary"),
                                             vmem_limit_bytes=VMEM_LIMIT_BYTES),
        name=name,
    )(x, g_ffn.reshape(1, -1), w1, w2)


def _ple_kernel(x_ref, p_ref, g_ple_ref, w_pg_ref, w_pe_ref, g_final_ref, o_ref):
    x = x_ref[...]
    gate = jax.nn.sigmoid(jnp.dot(_rms(x, g_ple_ref[...]).astype(bf16), pltpu.bitcast(w_pg_ref[...], bf16),
                                  preferred_element_type=f32))
    emb = jnp.dot(p_ref[...].astype(bf16), pltpu.bitcast(w_pe_ref[...], bf16), preferred_element_type=f32)
    o_ref[...] = _rms(x + emb * gate, g_final_ref[...])


def _ple_call(x, p, g_ple, w_pg, w_pe, g_final, *, rows, name):
    n = x.shape[0]
    return pl.pallas_call(
        _ple_kernel,
        grid=(n // rows,),
        in_specs=[pl.BlockSpec((rows, D_MODEL), lambda i: (i, 0)),
                  pl.BlockSpec((rows, D_PLE), lambda i: (i, 0)),
                  _const_spec((1, D_MODEL)),
                  _const_spec((D_MODEL // 2, D_MODEL)),
                  _const_spec((D_PLE // 2, D_MODEL)),
                  _const_spec((1, D_MODEL))],
        out_specs=pl.BlockSpec((rows, D_MODEL), lambda i: (i, 0)),
        out_shape=jax.ShapeDtypeStruct((n, D_MODEL), f32),
        compiler_params=pltpu.CompilerParams(dimension_semantics=("arbitrary",),
                                             vmem_limit_bytes=VMEM_LIMIT_BYTES),
        name=name,
    )(x, p, g_ple.reshape(1, -1), w_pg, w_pe, g_final.reshape(1, -1))


def kernel(x_prompt, x_sample, p_prompt, p_sample, cache_conv, g_mix, w_in, b_in, g_v, b_v, w_s, b_s, w_dw, b_dw, g_c, b_c, g_oa, g_ob, w_out, g_ffn, w1, w2, g_ple, w_pg, w_pe, g_final):
    depth, batch, seq, _ = p_prompt.shape
    _, dec_batch, dec_seq, _ = p_sample.shape
    assert depth == 1 and batch == 1
    assert seq % MIX_ROWS == 0 and seq % FFN_ROWS == 0 and seq % PLE_ROWS == 0
    assert dec_batch == SEGMENTS and dec_batch * dec_seq == MLP_CHUNK and dec_seq % SUBLANES == 0
    assert dec_seq <= CHUNK

    lw = dict(g_mix=g_mix[0], w_in=_pack_rows(w_in[0], "pack_w_in"), b_in=b_in[0], g_v=g_v[0], b_v=b_v[0],
              w_dw=w_dw[0], b_dw=b_dw[0], g_c=g_c[0], b_c=b_c[0], g_oa=g_oa[0], g_ob=g_ob[0],
              w_out=_pack_rows(w_out[0], "pack_w_out"))
    w1b, w2b = _pack_rows(w1[0], "pack_w1"), _pack_rows(w2[0], "pack_w2")
    w_pgb, w_peb = _pack_rows(w_pg[0], "pack_w_pg"), _pack_rows(w_pe[0], "pack_w_pe")
    ws, bs = w_s[0], b_s[0]

    bs_prompt = jnp.repeat(bs.T, HEAD_DIM_A, axis=1)
    hist0 = jnp.zeros((1, HIST, D_B), f32)
    xp, hist_p = _mix_call(x_prompt.reshape(seq, D_MODEL), hist0, ws, bs_prompt, lw,
                           rows=MIX_ROWS, chained=True, mask_ws=True, emit_v=False, pipelined=True)
    xp = _ffn_call(xp, g_ffn[0], w1b, w2b, rows=FFN_ROWS, name="ffn_prompt")
    yp = _ple_call(xp, p_prompt.reshape(seq, D_PLE), g_ple[0], w_pgb, w_peb, g_final,
                   rows=PLE_ROWS, name="ple_prompt")

    qi = jnp.arange(MLP_CHUNK)
    keep = (qi[None, :] // CHUNK) <= (qi[:, None] // CHUNK)
    ws_head = jnp.where(keep[None], ws, jnp.zeros_like(ws))[:, :dec_seq, :dec_seq]
    ws_sample = jnp.einsum("st,hij->hsitj", jnp.eye(dec_batch, dtype=f32), ws_head)
    ws_sample = ws_sample.reshape(N_HEADS_A, MLP_CHUNK, MLP_CHUNK)
    bs_sample = jnp.tile(jnp.repeat(bs[:, :dec_seq].T, HEAD_DIM_A, axis=1), (dec_batch, 1))
    hist_s0 = jnp.pad(cache_conv[0], ((0, 0), (HIST_PAD, 0), (0, 0)))
    rows_s = dec_batch * dec_seq
    xs, hist_s, v_s = _mix_call(x_sample.reshape(rows_s, D_MODEL), hist_s0, ws_sample, bs_sample, lw,
                                rows=rows_s, chained=False, mask_ws=False, emit_v=True, pipelined=False)
    xs = _ffn_call(xs, g_ffn[0], w1b, w2b, rows=rows_s, name="ffn_sample")
    ys = _ple_call(xs, p_sample.reshape(rows_s, D_PLE), g_ple[0], w_pgb, w_peb, g_final,
                   rows=rows_s, name="ple_sample")

    return (yp.reshape(batch, seq, D_MODEL),
            ys.reshape(dec_batch, dec_seq, D_MODEL),
            hist_p[:, HIST_PAD:, :][None],
            hist_s[:, HIST_PAD:, :][None],
            v_s.reshape(1, dec_batch, dec_seq, D_A))
```

```python
import functools

import jax
import jax.numpy as jnp
from jax import lax
from jax.experimental import pallas as pl
from jax.experimental.pallas import tpu as pltpu

D_MODEL = 2048
D_A = 1024
D_B = 1024
N_HEADS_A = 8
HEAD_DIM_A = D_A // N_HEADS_A
MLP_CHUNK = 128
CHUNK = 64
CONV_WIDTH = 31
D_FF = 4 * D_MODEL
D_PLE = 256
EPS = 1e-6

SUBLANES = 8
LANES = 128
BF16_SUBLANES = 16
SEGMENTS = SUBLANES
HIST = 32
HIST_PAD = HIST - (CONV_WIDTH - 1)
VMEM_LIMIT_BYTES = 60 * 1024 * 1024

MIX_ROWS = 256
FFN_ROWS = 512
FFN_COLS = 1024
PLE_ROWS = 512
PACK_BLOCK_BYTES = 4 * 1024 * 1024
SLABS_PER_STAGE = 1
IN_PROJ_PIECES = 8
OUT_PROJ_PIECES = 4
N_MIX_PARAMS = 15

f32 = jnp.float32
bf16 = jnp.bfloat16


def _rms(x, g):
    return x * lax.rsqrt(jnp.mean(x * x, axis=-1, keepdims=True) + EPS) * g


def _ln(x, g, b):
    mu = jnp.mean(x, axis=-1, keepdims=True)
    xc = x - mu
    var = jnp.mean(xc * xc, axis=-1, keepdims=True)
    return xc * lax.rsqrt(var + EPS) * g + b


def _pitch(n):
    return ((-(-n // SUBLANES)) | 1) * SUBLANES


_SLABS = [slice(l * LANES, (l + 1) * LANES) for l in range(D_B // LANES)]


def _load_conv_history(hist_in_ref, stage_ref, *, seg, chained):
    pin = _pitch(HIST + seg)
    for r in range(1 if chained else SEGMENTS):
        for l, lanes in enumerate(_SLABS):
            stage_ref[l, r * pin:r * pin + HIST, :] = hist_in_ref[r, :, lanes]


def _stage_conv_taps(glu, stage_ref, taps_ref, *, chained):
    rows = glu.shape[0]
    seg = rows // SEGMENTS
    pin = _pitch(HIST + seg)
    for r in range(SEGMENTS):
        for l, lanes in enumerate(_SLABS):
            stage_ref[l, r * pin + HIST:r * pin + HIST + seg, :] = glu[r * seg:(r + 1) * seg, lanes]
            if chained and r > 0:
                stage_ref[l, r * pin:r * pin + HIST, :] = glu[r * seg - HIST:r * seg, lanes]
    for q in range(seg + CONV_WIDTH - 1):
        for l in range(len(_SLABS)):
            taps_ref[l, q * SEGMENTS:(q + 1) * SEGMENTS, :] = (
                stage_ref[l, pl.ds(HIST_PAD + q, SEGMENTS, stride=pin), :])
    if chained:
        for l, lanes in enumerate(_SLABS):
            stage_ref[l, 0:HIST, :] = glu[rows - HIST:, lanes]


def _conv_slab(l, rows, w_dw_ref, b_dw_ref, taps_ref, unperm_ref, after):
    seg = rows // SEGMENTS
    pout = _pitch(seg)
    lanes = _SLABS[l]
    c = jnp.broadcast_to(after(b_dw_ref[:, lanes]), (rows, LANES))
    for k in range(CONV_WIDTH):
        c = c + after(w_dw_ref[k:k + 1, lanes]) * taps_ref[l, k * SEGMENTS:k * SEGMENTS + rows, :]
    for b in range(seg):
        unperm_ref[l, pl.ds(b, SEGMENTS, stride=pout), :] = c[b * SEGMENTS:(b + 1) * SEGMENTS, :]
    return jnp.concatenate([unperm_ref[l, r * pout:r * pout + seg, :] for r in range(SEGMENTS)], axis=0)


def _new_conv_history(glu, hist_in_ref, *, chained):
    rows = glu.shape[0]
    seg = rows // SEGMENTS
    if chained:
        return glu[rows - HIST:, :][None]
    if seg >= HIST:
        return jnp.stack([glu[(r + 1) * seg - HIST:(r + 1) * seg, :] for r in range(SEGMENTS)])
    return jnp.stack([jnp.concatenate([hist_in_ref[r, seg:, :], glu[r * seg:(r + 1) * seg, :]], axis=0)
                      for r in range(SEGMENTS)])


def _zero_of(val):
    last = val[val.shape[0] - SUBLANES:, val.shape[1] - LANES:]
    return jnp.where(jnp.isfinite(last), last, 0.0) * 0.0


def _after(tie, row):
    return row if tie is None else row + jnp.tile(tie, (1, row.shape[1] // LANES))


class _Interleave:
    def __init__(self, pieces):
        self._pieces = list(pieces)
        self._tie = None

    def next(self, stage_out=None):
        prev = self._tie
        self._tie = None
        if self._pieces:
            offset = None
            if stage_out is not None:
                offset = pl.multiple_of(_zero_of(stage_out).astype(jnp.int32)[0, 0], BF16_SUBLANES)
            self._tie = _zero_of(self._pieces.pop(0)(offset))[0:1, :]
        return prev

    def flush(self):
        while self._pieces:
            self._pieces.pop(0)(None)


def _mixers(z_of, sched, hist_in_ref, g_v_ref, b_v_ref, ws_ref, bs_ref, w_dw_ref, b_dw_ref, g_c_ref, b_c_ref,
            g_oa_ref, g_ob_ref, hist_out_ref, v_ref, stage_ref, taps_ref, unperm_ref, *, rows, chained, mask_ws,
            first_stage=None):
    tie = sched.next()
    if first_stage is not None:
        first_stage()
    u = jax.nn.gelu(z_of(0, D_A, tie))
    tie = sched.next(u)
    v = _ln(jax.nn.gelu(z_of(D_A, 2 * D_A, tie)), g_v_ref[...], b_v_ref[...])
    if v_ref is not None:
        v_ref[...] = v
    ws = ws_ref[...]
    if mask_ws:
        qi = lax.broadcasted_iota(jnp.int32, (MLP_CHUNK, MLP_CHUNK), 0)
        ki = lax.broadcasted_iota(jnp.int32, (MLP_CHUNK, MLP_CHUNK), 1)
        keep = (ki // CHUNK) <= (qi // CHUNK)
        ws = jnp.where(keep[None], ws, jnp.zeros_like(ws))
    ws = ws.astype(bf16)
    vb = v.astype(bf16)
    mixed_chunks = []
    for c in range(rows // MLP_CHUNK):
        r0 = c * MLP_CHUNK
        heads = [
            jnp.dot(ws[hd], vb[r0:r0 + MLP_CHUNK, hd * HEAD_DIM_A:(hd + 1) * HEAD_DIM_A],
                    preferred_element_type=f32)
            for hd in range(N_HEADS_A)
        ]
        mixed_chunks.append(jnp.concatenate(heads, axis=1) + bs_ref[...])
    mixed = mixed_chunks[0] if len(mixed_chunks) == 1 else jnp.concatenate(mixed_chunks, axis=0)
    tie = sched.next(v)
    y_a = _rms(u * mixed, _after(tie, g_oa_ref[...]))

    tie = sched.next(y_a)
    glu = z_of(2 * D_A, 2 * D_A + D_B, tie) * jax.nn.sigmoid(z_of(2 * D_A + D_B, 2 * D_A + 2 * D_B, tie))
    hist_out_ref[...] = _new_conv_history(glu, hist_in_ref, chained=chained)
    _stage_conv_taps(glu, stage_ref, taps_ref, chained=chained)
    conv_slabs = []
    stage_out = glu
    for l in range(len(_SLABS)):
        if l % SLABS_PER_STAGE == 0:
            tie = sched.next(stage_out)
        conv_slabs.append(_conv_slab(l, rows, w_dw_ref, b_dw_ref, taps_ref, unperm_ref,
                                     functools.partial(_after, tie)))
        stage_out = conv_slabs[-1]
    tie = sched.next(stage_out)
    c = jnp.concatenate(conv_slabs, axis=1)
    y_b = _rms(jax.nn.silu(_ln(c, g_c_ref[...], _after(tie, b_c_ref[...]))), g_ob_ref[...])
    return y_a.astype(bf16), y_b.astype(bf16)


def _mix_step(x_ref, x_out_ref, z_ref, z_next_ref, y_ref, y_next_ref, hn_ref, hist_in_ref, g_mix_ref, w_in_ref,
              b_in_ref, g_v_ref, b_v_ref, ws_ref, bs_ref, w_dw_ref, b_dw_ref, g_c_ref, b_c_ref, g_oa_ref,
              g_ob_ref, w_out_ref, o_ref, hist_out_ref, v_ref, stage_ref, taps_ref, unperm_ref, *, chained,
              mask_ws):
    rows = x_ref.shape[0]
    out_cols = D_MODEL // OUT_PROJ_PIECES
    in_cols = (2 * D_A + 2 * D_B) // IN_PROJ_PIECES

    def lhs_of(ref, offset):
        return ref[...] if offset is None else ref[pl.ds(offset, rows), :]

    def out_proj_piece(j, offset):
        cols = slice(j * out_cols, (j + 1) * out_cols)
        oj = jnp.dot(lhs_of(y_ref, offset), pltpu.bitcast(w_out_ref[:, cols], bf16), preferred_element_type=f32)
        o_ref[:, cols] = x_out_ref[:, cols] + oj
        return oj

    def in_proj_piece(j, offset):
        cols = slice(j * in_cols, (j + 1) * in_cols)
        zj = jnp.dot(lhs_of(hn_ref, offset), pltpu.bitcast(w_in_ref[:, cols], bf16), preferred_element_type=f32)
        z_next_ref[:, cols] = zj
        return zj

    def norm_next_tile():
        hn_ref[...] = _rms(x_ref[...], g_mix_ref[...]).astype(bf16)

    sched = _Interleave([functools.partial(out_proj_piece, j) for j in range(OUT_PROJ_PIECES)]
                        + [functools.partial(in_proj_piece, j) for j in range(IN_PROJ_PIECES)])
    z_of = lambda lo, hi, tie: z_ref[:, lo:hi] + _after(tie, b_in_ref[:, lo:hi])
    y_a, y_b = _mixers(z_of, sched, hist_in_ref, g_v_ref, b_v_ref, ws_ref, bs_ref, w_dw_ref, b_dw_ref, g_c_ref,
                       b_c_ref, g_oa_ref, g_ob_ref, hist_out_ref, v_ref, stage_ref, taps_ref, unperm_ref,
                       rows=rows, chained=chained, mask_ws=mask_ws, first_stage=norm_next_tile)
    y_next_ref[:, 0:D_A] = y_a
    y_next_ref[:, D_A:] = y_b
    sched.flush()


def _mix_single(x_ref, hist_in_ref, g_mix_ref, w_in_ref, b_in_ref, g_v_ref, b_v_ref, ws_ref, bs_ref, w_dw_ref,
                b_dw_ref, g_c_ref, b_c_ref, g_oa_ref, g_ob_ref, w_out_ref, o_ref, hist_out_ref, v_ref, stage_ref,
                taps_ref, unperm_ref, *, chained, mask_ws):
    rows = x_ref.shape[0]
    _load_conv_history(hist_in_ref, stage_ref, seg=rows // SEGMENTS, chained=chained)
    x = x_ref[...]
    z = jnp.dot(_rms(x, g_mix_ref[...]).astype(bf16), pltpu.bitcast(w_in_ref[...], bf16),
                preferred_element_type=f32)
    z_of = lambda lo, hi, tie: z[:, lo:hi] + b_in_ref[:, lo:hi]
    y_a, y_b = _mixers(z_of, _Interleave([]), hist_in_ref, g_v_ref, b_v_ref, ws_ref, bs_ref, w_dw_ref, b_dw_ref,
                       g_c_ref, b_c_ref, g_oa_ref, g_ob_ref, hist_out_ref, v_ref, stage_ref, taps_ref, unperm_ref,
                       rows=rows, chained=chained, mask_ws=mask_ws)
    y = jnp.concatenate([y_a, y_b], axis=1)
    o_ref[...] = x + jnp.dot(y, pltpu.bitcast(w_out_ref[...], bf16), preferred_element_type=f32)


def _mix_kernel(*refs, chained, mask_ws, emit_v, pipelined):
    refs = list(refs)
    x_ref = refs.pop(0)
    x_out_ref = refs.pop(0) if pipelined else None
    params = refs[:N_MIX_PARAMS]
    hist_in_ref = params[0]
    outs = refs[N_MIX_PARAMS:]
    o_ref, hist_out_ref = outs[:2]
    v_ref = outs[2] if emit_v else None
    stage_ref, taps_ref, unperm_ref = outs[-3:]
    rows = x_ref.shape[0]
    tail = (o_ref, hist_out_ref, v_ref, stage_ref, taps_ref, unperm_ref)

    if not pipelined:
        _mix_single(x_ref, *params, *tail, chained=chained, mask_ws=mask_ws)
        return

    z0_ref, z1_ref, y0_ref, y1_ref, hn_ref = outs[-8:-3]
    step = pl.program_id(0)
    body = functools.partial(_mix_step, chained=chained, mask_ws=mask_ws)

    @pl.when(step == 0)
    def _():
        z1_ref[...] = jnp.zeros_like(z1_ref)
        y0_ref[...] = jnp.zeros_like(y0_ref)
        stage_ref[...] = jnp.zeros_like(stage_ref)

    @pl.when(step == 1)
    def _():
        _load_conv_history(hist_in_ref, stage_ref, seg=rows // SEGMENTS, chained=chained)

    @pl.when(step % 2 == 0)
    def _():
        body(x_ref, x_out_ref, z1_ref, z0_ref, y0_ref, y1_ref, hn_ref, *params, *tail)

    @pl.when(step % 2 == 1)
    def _():
        body(x_ref, x_out_ref, z0_ref, z1_ref, y1_ref, y0_ref, hn_ref, *params, *tail)


def _pack_kernel(w_ref, o_ref):
    o_ref[...] = pltpu.bitcast(w_ref[...].astype(bf16), jnp.uint32)


def _pack_rows(w, name):
    k, n = w.shape
    rows = max(BF16_SUBLANES, min(k, PACK_BLOCK_BYTES // (4 * n)))
    assert k % rows == 0 and rows % BF16_SUBLANES == 0
    return pl.pallas_call(
        _pack_kernel,
        grid=(k // rows,),
        in_specs=[pl.BlockSpec((rows, n), lambda i: (i, 0))],
        out_specs=pl.BlockSpec((rows // 2, n), lambda i: (i, 0)),
        out_shape=jax.ShapeDtypeStruct((k // 2, n), jnp.uint32),
        compiler_params=pltpu.CompilerParams(dimension_semantics=("arbitrary",),
                                             vmem_limit_bytes=VMEM_LIMIT_BYTES),
        name=name,
    )(w)


def _const_spec(shape):
    zeros = (0,) * len(shape)
    return pl.BlockSpec(shape, lambda *_: zeros, pipeline_mode=pl.Buffered(1))


def _mix_call(x, hist_in, ws, bs, lw, *, rows, chained, mask_ws, emit_v, pipelined):
    n = x.shape[0]
    tiles = n // rows
    streams = hist_in.shape[0]
    seg = rows // SEGMENTS
    assert streams == (1 if chained else SEGMENTS) and (seg >= HIST or not chained)
    assert pipelined or tiles == 1
    n_slabs = D_B // LANES
    vec = lambda a: a.reshape(1, -1)
    params = [hist_in, vec(lw["g_mix"]), lw["w_in"], vec(lw["b_in"]), vec(lw["g_v"]), vec(lw["b_v"]),
              ws, bs, lw["w_dw"], vec(lw["b_dw"]), vec(lw["g_c"]), vec(lw["b_c"]),
              vec(lw["g_oa"]), vec(lw["g_ob"]), lw["w_out"]]
    assert len(params) == N_MIX_PARAMS
    scratch = [pltpu.VMEM((n_slabs, SEGMENTS * _pitch(HIST + seg), LANES), f32),
               pltpu.VMEM((n_slabs, (seg + CONV_WIDTH - 1) * SEGMENTS, LANES), f32),
               pltpu.VMEM((n_slabs, SEGMENTS * _pitch(seg), LANES), f32)]
    if pipelined:
        grid = (tiles + 2,)
        rows_of = lambda cols, idx: pl.BlockSpec((rows, cols), lambda s: (idx(s), 0))
        cur = lambda s: jnp.minimum(s, tiles - 1)
        done = lambda s: jnp.maximum(s - 2, 0)
        operands = [x, x] + params
        in_specs = [rows_of(D_MODEL, cur), rows_of(D_MODEL, done)]
        out_rows = lambda cols: rows_of(cols, done)
        z_buf = pltpu.VMEM((rows, 2 * D_A + 2 * D_B), f32)
        y_buf = pltpu.VMEM((rows, D_A + D_B), bf16)
        scratch = [z_buf, z_buf, y_buf, y_buf, pltpu.VMEM((rows, D_MODEL), bf16)] + scratch
    else:
        grid = (tiles,)
        operands = [x] + params
        out_rows = lambda cols: pl.BlockSpec((rows, cols), lambda s: (s, 0))
        in_specs = [out_rows(D_MODEL)]
    in_specs += [_const_spec(a.shape) for a in params]
    out_shape = [jax.ShapeDtypeStruct((n, D_MODEL), f32),
                 jax.ShapeDtypeStruct((streams, HIST, D_B), f32)]
    out_specs = [out_rows(D_MODEL), _const_spec((streams, HIST, D_B))]
    if emit_v:
        out_shape.append(jax.ShapeDtypeStruct((n, D_A), f32))
        out_specs.append(out_rows(D_A))
    return pl.pallas_call(
        functools.partial(_mix_kernel, chained=chained, mask_ws=mask_ws, emit_v=emit_v, pipelined=pipelined),
        grid=grid,
        in_specs=in_specs,
        out_specs=out_specs,
        out_shape=out_shape,
        scratch_shapes=scratch,
        compiler_params=pltpu.CompilerParams(dimension_semantics=("arbitrary",),
                                             vmem_limit_bytes=VMEM_LIMIT_BYTES),
        name="mix_prompt" if pipelined else "mix_sample",
    )(*operands)


def _ffn_kernel(x_ref, g_ref, w1_ref, w2_ref, o_ref, hn_ref, acc_ref):
    f = pl.program_id(1)

    @pl.when(f == 0)
    def _():
        hn_ref[...] = _rms(x_ref[...], g_ref[...]).astype(bf16)
        acc_ref[...] = jnp.zeros_like(acc_ref)

    h = jnp.dot(hn_ref[...], pltpu.bitcast(w1_ref[...], bf16), preferred_element_type=f32)
    h = jnp.square(jnp.maximum(h, 0.0)).astype(bf16)
    acc_ref[...] += jnp.dot(h, pltpu.bitcast(w2_ref[...], bf16), preferred_element_type=f32)

    @pl.when(f == pl.num_programs(1) - 1)
    def _():
        o_ref[...] = x_ref[...] + acc_ref[...]


def _ffn_call(x, g_ffn, w1, w2, *, rows, name):
    n = x.shape[0]
    return pl.pallas_call(
        _ffn_kernel,
        grid=(n // rows, D_FF // FFN_COLS),
        in_specs=[pl.BlockSpec((rows, D_MODEL), lambda i, f: (i, 0)),
                  _const_spec((1, D_MODEL)),
                  pl.BlockSpec((D_MODEL // 2, FFN_COLS), lambda i, f: (0, f)),
                  pl.BlockSpec((FFN_COLS // 2, D_MODEL), lambda i, f: (f, 0))],
        out_specs=pl.BlockSpec((rows, D_MODEL), lambda i, f: (i, 0)),
        out_shape=jax.ShapeDtypeStruct((n, D_MODEL), f32),
        scratch_shapes=[pltpu.VMEM((rows, D_MODEL), bf16), pltpu.VMEM((rows, D_MODEL), f32)],
        compiler_params=pltpu.CompilerParams(dimension_semantics=("arbitrary", "arbitrary"),
                                             vmem_limit_bytes=VMEM_LIMIT_BYTES),
        name=name,
    )(x, g_ffn.reshape(1, -1), w1, w2)


def _ple_kernel(x_ref, p_ref, g_ple_ref, w_pg_ref, w_pe_ref, g_final_ref, o_ref):
    x = x_ref[...]
    gate = jax.nn.sigmoid(jnp.dot(_rms(x, g_ple_ref[...]).astype(bf16), pltpu.bitcast(w_pg_ref[...], bf16),
                                  preferred_element_type=f32))
    emb = jnp.dot(p_ref[...].astype(bf16), pltpu.bitcast(w_pe_ref[...], bf16), preferred_element_type=f32)
    o_ref[...] = _rms(x + emb * gate, g_final_ref[...])


def _ple_call(x, p, g_ple, w_pg, w_pe, g_final, *, rows, name):
    n = x.shape[0]
    return pl.pallas_call(
        _ple_kernel,
        grid=(n // rows,),
        in_specs=[pl.BlockSpec((rows, D_MODEL), lambda i: (i, 0)),
                  pl.BlockSpec((rows, D_PLE), lambda i: (i, 0)),
                  _const_spec((1, D_MODEL)),
                  _const_spec((D_MODEL // 2, D_MODEL)),
                  _const_spec((D_PLE // 2, D_MODEL)),
                  _const_spec((1, D_MODEL))],
        out_specs=pl.BlockSpec((rows, D_MODEL), lambda i: (i, 0)),
        out_shape=jax.ShapeDtypeStruct((n, D_MODEL), f32),
        compiler_params=pltpu.CompilerParams(dimension_semantics=("arbitrary",),
                                             vmem_limit_bytes=VMEM_LIMIT_BYTES),
        name=name,
    )(x, p, g_ple.reshape(1, -1), w_pg, w_pe, g_final.reshape(1, -1))


def kernel(x_prompt, x_sample, p_prompt, p_sample, cache_conv, g_mix, w_in, b_in, g_v, b_v, w_s, b_s, w_dw, b_dw, g_c, b_c, g_oa, g_ob, w_out, g_ffn, w1, w2, g_ple, w_pg, w_pe, g_final):
    depth, batch, seq, _ = p_prompt.shape
    _, dec_batch, dec_seq, _ = p_sample.shape
    assert depth == 1 and batch == 1
    assert seq % MIX_ROWS == 0 and seq % FFN_ROWS == 0 and seq % PLE_ROWS == 0
    assert dec_batch == SEGMENTS and dec_batch * dec_seq == MLP_CHUNK and dec_seq % SUBLANES == 0
    assert dec_seq <= CHUNK

    lw = dict(g_mix=g_mix[0], w_in=_pack_rows(w_in[0], "pack_w_in"), b_in=b_in[0], g_v=g_v[0], b_v=b_v[0],
              w_dw=w_dw[0], b_dw=b_dw[0], g_c=g_c[0], b_c=b_c[0], g_oa=g_oa[0], g_ob=g_ob[0],
              w_out=_pack_rows(w_out[0], "pack_w_out"))
    w1b, w2b = _pack_rows(w1[0], "pack_w1"), _pack_rows(w2[0], "pack_w2")
    w_pgb, w_peb = _pack_rows(w_pg[0], "pack_w_pg"), _pack_rows(w_pe[0], "pack_w_pe")
    ws, bs = w_s[0], b_s[0]

    bs_prompt = jnp.repeat(bs.T, HEAD_DIM_A, axis=1)
    hist0 = jnp.zeros((1, HIST, D_B), f32)
    xp, hist_p = _mix_call(x_prompt.reshape(seq, D_MODEL), hist0, ws, bs_prompt, lw,
                           rows=MIX_ROWS, chained=True, mask_ws=True, emit_v=False, pipelined=True)
    xp = _ffn_call(xp, g_ffn[0], w1b, w2b, rows=FFN_ROWS, name="ffn_prompt")
    yp = _ple_call(xp, p_prompt.reshape(seq, D_PLE), g_ple[0], w_pgb, w_peb, g_final,
                   rows=PLE_ROWS, name="ple_prompt")

    qi = jnp.arange(MLP_CHUNK)
    keep = (qi[None, :] // CHUNK) <= (qi[:, None] // CHUNK)
    ws_head = jnp.where(keep[None], ws, jnp.zeros_like(ws))[:, :dec_seq, :dec_seq]
    ws_sample = jnp.einsum("st,hij->hsitj", jnp.eye(dec_batch, dtype=f32), ws_head)
    ws_sample = ws_sample.reshape(N_HEADS_A, MLP_CHUNK, MLP_CHUNK)
    bs_sample = jnp.tile(jnp.repeat(bs[:, :dec_seq].T, HEAD_DIM_A, axis=1), (dec_batch, 1))
    hist_s0 = jnp.pad(cache_conv[0], ((0, 0), (HIST_PAD, 0), (0, 0)))
    rows_s = dec_batch * dec_seq
    xs, hist_s, v_s = _mix_call(x_sample.reshape(rows_s, D_MODEL), hist_s0, ws_sample, bs_sample, lw,
                                rows=rows_s, chained=False, mask_ws=False, emit_v=True, pipelined=False)
    xs = _ffn_call(xs, g_ffn[0], w1b, w2b, rows=rows_s, name="ffn_sample")
    ys = _ple_call(xs, p_sample.reshape(rows_s, D_PLE), g_ple[0], w_pgb, w_peb, g_final,
                   rows=rows_s, name="ple_sample")

    return (yp.reshape(batch, seq, D_MODEL),
            ys.reshape(dec_batch, dec_seq, D_MODEL),
            hist_p[:, HIST_PAD:, :][None],
            hist_s[:, HIST_PAD:, :][None],
            v_s.reshape(1, dec_batch, dec_seq, D_A))
```

```python
import functools

import jax
import jax.numpy as jnp
from jax import lax
from jax.experimental import pallas as pl
from jax.experimental.pallas import tpu as pltpu

D_MODEL = 2048
D_A = 1024
D_B = 1024
N_HEADS_A = 8
HEAD_DIM_A = D_A // N_HEADS_A
MLP_CHUNK = 128
CHUNK = 64
CONV_WIDTH = 31
D_FF = 4 * D_MODEL
D_PLE = 256
EPS = 1e-6

SUBLANES = 8
LANES = 128
BF16_SUBLANES = 16
SEGMENTS = SUBLANES
HIST = 32
HIST_PAD = HIST - (CONV_WIDTH - 1)
VMEM_LIMIT_BYTES = 60 * 1024 * 1024

MIX_ROWS = 256
FFN_ROWS = 1024
FFN_COLS = 1024
PLE_ROWS = 512
PACK_BLOCK_BYTES = 4 * 1024 * 1024
IN_PROJ_PIECES = 8
OUT_PROJ_PIECES = 4
N_MIX_PARAMS = 15

f32 = jnp.float32
bf16 = jnp.bfloat16


def _rms(x, g):
    return x * lax.rsqrt(jnp.mean(x * x, axis=-1, keepdims=True) + EPS) * g


def _ln(x, g, b):
    mu = jnp.mean(x, axis=-1, keepdims=True)
    xc = x - mu
    var = jnp.mean(xc * xc, axis=-1, keepdims=True)
    return xc * lax.rsqrt(var + EPS) * g + b


def _pitch(n):
    return ((-(-n // SUBLANES)) | 1) * SUBLANES


_SLABS = [slice(l * LANES, (l + 1) * LANES) for l in range(D_B // LANES)]


def _load_conv_history(hist_in_ref, stage_ref, *, seg, chained):
    pin = _pitch(HIST + seg)
    for r in range(1 if chained else SEGMENTS):
        for l, lanes in enumerate(_SLABS):
            stage_ref[l, r * pin:r * pin + HIST, :] = hist_in_ref[r, :, lanes]


def _stage_conv_taps(glu, stage_ref, taps_ref, *, chained):
    rows = glu.shape[0]
    seg = rows // SEGMENTS
    pin = _pitch(HIST + seg)
    for r in range(SEGMENTS):
        for l, lanes in enumerate(_SLABS):
            stage_ref[l, r * pin + HIST:r * pin + HIST + seg, :] = glu[r * seg:(r + 1) * seg, lanes]
            if chained and r > 0:
                stage_ref[l, r * pin:r * pin + HIST, :] = glu[r * seg - HIST:r * seg, lanes]
    for q in range(seg + CONV_WIDTH - 1):
        for l in range(len(_SLABS)):
            taps_ref[l, q * SEGMENTS:(q + 1) * SEGMENTS, :] = (
                stage_ref[l, pl.ds(HIST_PAD + q, SEGMENTS, stride=pin), :])
    if chained:
        for l, lanes in enumerate(_SLABS):
            stage_ref[l, 0:HIST, :] = glu[rows - HIST:, lanes]


def _conv_slab(l, rows, w_dw_ref, b_dw_ref, taps_ref, unperm_ref, after):
    seg = rows // SEGMENTS
    pout = _pitch(seg)
    lanes = _SLABS[l]
    c = jnp.broadcast_to(after(b_dw_ref[:, lanes]), (rows, LANES))
    for k in range(CONV_WIDTH):
        c = c + after(w_dw_ref[k:k + 1, lanes]) * taps_ref[l, k * SEGMENTS:k * SEGMENTS + rows, :]
    for b in range(seg):
        unperm_ref[l, pl.ds(b, SEGMENTS, stride=pout), :] = c[b * SEGMENTS:(b + 1) * SEGMENTS, :]
    return jnp.concatenate([unperm_ref[l, r * pout:r * pout + seg, :] for r in range(SEGMENTS)], axis=0)


def _new_conv_history(glu, hist_in_ref, *, chained):
    rows = glu.shape[0]
    seg = rows // SEGMENTS
    if chained:
        return glu[rows - HIST:, :][None]
    if seg >= HIST:
        return jnp.stack([glu[(r + 1) * seg - HIST:(r + 1) * seg, :] for r in range(SEGMENTS)])
    return jnp.stack([jnp.concatenate([hist_in_ref[r, seg:, :], glu[r * seg:(r + 1) * seg, :]], axis=0)
                      for r in range(SEGMENTS)])


def _zero_of(val):
    last = val[val.shape[0] - SUBLANES:, val.shape[1] - LANES:]
    return jnp.where(jnp.isfinite(last), last, 0.0) * 0.0


def _after(tie, row):
    return row if tie is None else row + jnp.tile(tie, (1, row.shape[1] // LANES))


class _Interleave:
    def __init__(self, pieces):
        self._pieces = list(pieces)
        self._tie = None

    def next(self, stage_out=None):
        prev = self._tie
        self._tie = None
        if self._pieces:
            offset = None
            if stage_out is not None:
                offset = pl.multiple_of(_zero_of(stage_out).astype(jnp.int32)[0, 0], BF16_SUBLANES)
            self._tie = _zero_of(self._pieces.pop(0)(offset))[0:1, :]
        return prev

    def flush(self):
        while self._pieces:
            self._pieces.pop(0)(None)


def _mixers(z_of, sched, hist_in_ref, g_v_ref, b_v_ref, ws_ref, bs_ref, w_dw_ref, b_dw_ref, g_c_ref, b_c_ref,
            g_oa_ref, g_ob_ref, hist_out_ref, v_ref, stage_ref, taps_ref, unperm_ref, *, rows, chained, mask_ws,
            first_stage=None):
    tie = sched.next()
    if first_stage is not None:
        first_stage()
    u = jax.nn.gelu(z_of(0, D_A, tie))
    tie = sched.next(u)
    v = _ln(jax.nn.gelu(z_of(D_A, 2 * D_A, tie)), g_v_ref[...], b_v_ref[...])
    if v_ref is not None:
        v_ref[...] = v
    ws = ws_ref[...]
    if mask_ws:
        qi = lax.broadcasted_iota(jnp.int32, (MLP_CHUNK, MLP_CHUNK), 0)
        ki = lax.broadcasted_iota(jnp.int32, (MLP_CHUNK, MLP_CHUNK), 1)
        keep = (ki // CHUNK) <= (qi // CHUNK)
        ws = jnp.where(keep[None], ws, jnp.zeros_like(ws))
    ws = ws.astype(bf16)
    vb = v.astype(bf16)
    mixed_chunks = []
    for c in range(rows // MLP_CHUNK):
        r0 = c * MLP_CHUNK
        heads = [
            jnp.dot(ws[hd], vb[r0:r0 + MLP_CHUNK, hd * HEAD_DIM_A:(hd + 1) * HEAD_DIM_A],
                    preferred_element_type=f32)
            for hd in range(N_HEADS_A)
        ]
        mixed_chunks.append(jnp.concatenate(heads, axis=1) + bs_ref[...])
    mixed = mixed_chunks[0] if len(mixed_chunks) == 1 else jnp.concatenate(mixed_chunks, axis=0)
    tie = sched.next(v)
    y_a = _rms(u * mixed, _after(tie, g_oa_ref[...]))

    tie = sched.next(y_a)
    glu = z_of(2 * D_A, 2 * D_A + D_B, tie) * jax.nn.sigmoid(z_of(2 * D_A + D_B, 2 * D_A + 2 * D_B, tie))
    hist_out_ref[...] = _new_conv_history(glu, hist_in_ref, chained=chained)
    _stage_conv_taps(glu, stage_ref, taps_ref, chained=chained)
    conv_slabs = []
    stage_out = glu
    for l in range(len(_SLABS)):
        tie = sched.next(stage_out)
        conv_slabs.append(_conv_slab(l, rows, w_dw_ref, b_dw_ref, taps_ref, unperm_ref,
                                     functools.partial(_after, tie)))
        stage_out = conv_slabs[-1]
    tie = sched.next(stage_out)
    c = jnp.concatenate(conv_slabs, axis=1)
    y_b = _rms(jax.nn.silu(_ln(c, g_c_ref[...], _after(tie, b_c_ref[...]))), g_ob_ref[...])
    return y_a.astype(bf16), y_b.astype(bf16)


def _mix_step(x_ref, x_out_ref, z_ref, z_next_ref, y_ref, y_next_ref, hn_ref, hist_in_ref, g_mix_ref, w_in_ref,
              b_in_ref, g_v_ref, b_v_ref, ws_ref, bs_ref, w_dw_ref, b_dw_ref, g_c_ref, b_c_ref, g_oa_ref,
              g_ob_ref, w_out_ref, o_ref, hist_out_ref, v_ref, stage_ref, taps_ref, unperm_ref, *, chained,
              mask_ws):
    rows = x_ref.shape[0]
    out_cols = D_MODEL // OUT_PROJ_PIECES
    in_cols = (2 * D_A + 2 * D_B) // IN_PROJ_PIECES

    def lhs_of(ref, offset):
        return ref[...] if offset is None else ref[pl.ds(offset, rows), :]

    def out_proj_piece(j, offset):
        cols = slice(j * out_cols, (j + 1) * out_cols)
        oj = jnp.dot(lhs_of(y_ref, offset), pltpu.bitcast(w_out_ref[:, cols], bf16), preferred_element_type=f32)
        o_ref[:, cols] = x_out_ref[:, cols] + oj
        return oj

    def in_proj_piece(j, offset):
        cols = slice(j * in_cols, (j + 1) * in_cols)
        zj = jnp.dot(lhs_of(hn_ref, offset), pltpu.bitcast(w_in_ref[:, cols], bf16), preferred_element_type=f32)
        z_next_ref[:, cols] = zj
        return zj

    def norm_next_tile():
        hn_ref[...] = _rms(x_ref[...], g_mix_ref[...]).astype(bf16)

    sched = _Interleave([functools.partial(out_proj_piece, j) for j in range(OUT_PROJ_PIECES)]
                        + [functools.partial(in_proj_piece, j) for j in range(IN_PROJ_PIECES)])
    z_of = lambda lo, hi, tie: z_ref[:, lo:hi] + _after(tie, b_in_ref[:, lo:hi])
    y_a, y_b = _mixers(z_of, sched, hist_in_ref, g_v_ref, b_v_ref, ws_ref, bs_ref, w_dw_ref, b_dw_ref, g_c_ref,
                       b_c_ref, g_oa_ref, g_ob_ref, hist_out_ref, v_ref, stage_ref, taps_ref, unperm_ref,
                       rows=rows, chained=chained, mask_ws=mask_ws, first_stage=norm_next_tile)
    y_next_ref[:, 0:D_A] = y_a
    y_next_ref[:, D_A:] = y_b
    sched.flush()


def _mix_single(x_ref, hist_in_ref, g_mix_ref, w_in_ref, b_in_ref, g_v_ref, b_v_ref, ws_ref, bs_ref, w_dw_ref,
                b_dw_ref, g_c_ref, b_c_ref, g_oa_ref, g_ob_ref, w_out_ref, o_ref, hist_out_ref, v_ref, stage_ref,
                taps_ref, unperm_ref, *, chained, mask_ws):
    rows = x_ref.shape[0]
    _load_conv_history(hist_in_ref, stage_ref, seg=rows // SEGMENTS, chained=chained)
    x = x_ref[...]
    z = jnp.dot(_rms(x, g_mix_ref[...]).astype(bf16), pltpu.bitcast(w_in_ref[...], bf16),
                preferred_element_type=f32)
    z_of = lambda lo, hi, tie: z[:, lo:hi] + b_in_ref[:, lo:hi]
    y_a, y_b = _mixers(z_of, _Interleave([]), hist_in_ref, g_v_ref, b_v_ref, ws_ref, bs_ref, w_dw_ref, b_dw_ref,
                       g_c_ref, b_c_ref, g_oa_ref, g_ob_ref, hist_out_ref, v_ref, stage_ref, taps_ref, unperm_ref,
                       rows=rows, chained=chained, mask_ws=mask_ws)
    y = jnp.concatenate([y_a, y_b], axis=1)
    o_ref[...] = x + jnp.dot(y, pltpu.bitcast(w_out_ref[...], bf16), preferred_element_type=f32)


def _mix_kernel(*refs, chained, mask_ws, emit_v, pipelined):
    refs = list(refs)
    x_ref = refs.pop(0)
    x_out_ref = refs.pop(0) if pipelined else None
    params = refs[:N_MIX_PARAMS]
    hist_in_ref = params[0]
    outs = refs[N_MIX_PARAMS:]
    o_ref, hist_out_ref = outs[:2]
    v_ref = outs[2] if emit_v else None
    stage_ref, taps_ref, unperm_ref = outs[-3:]
    rows = x_ref.shape[0]
    tail = (o_ref, hist_out_ref, v_ref, stage_ref, taps_ref, unperm_ref)

    if not pipelined:
        _mix_single(x_ref, *params, *tail, chained=chained, mask_ws=mask_ws)
        return

    z0_ref, z1_ref, y0_ref, y1_ref, hn_ref = outs[-8:-3]
    step = pl.program_id(0)
    body = functools.partial(_mix_step, chained=chained, mask_ws=mask_ws)

    @pl.when(step == 0)
    def _():
        z1_ref[...] = jnp.zeros_like(z1_ref)
        y0_ref[...] = jnp.zeros_like(y0_ref)
        stage_ref[...] = jnp.zeros_like(stage_ref)

    @pl.when(step == 1)
    def _():
        _load_conv_history(hist_in_ref, stage_ref, seg=rows // SEGMENTS, chained=chained)

    @pl.when(step % 2 == 0)
    def _():
        body(x_ref, x_out_ref, z1_ref, z0_ref, y0_ref, y1_ref, hn_ref, *params, *tail)

    @pl.when(step % 2 == 1)
    def _():
        body(x_ref, x_out_ref, z0_ref, z1_ref, y1_ref, y0_ref, hn_ref, *params, *tail)


def _pack_kernel(w_ref, o_ref):
    o_ref[...] = pltpu.bitcast(w_ref[...].astype(bf16), jnp.uint32)


def _pack_rows(w, name):
    k, n = w.shape
    rows = max(BF16_SUBLANES, min(k, PACK_BLOCK_BYTES // (4 * n)))
    assert k % rows == 0 and rows % BF16_SUBLANES == 0
    return pl.pallas_call(
        _pack_kernel,
        grid=(k // rows,),
        in_specs=[pl.BlockSpec((rows, n), lambda i: (i, 0))],
        out_specs=pl.BlockSpec((rows // 2, n), lambda i: (i, 0)),
        out_shape=jax.ShapeDtypeStruct((k // 2, n), jnp.uint32),
        compiler_params=pltpu.CompilerParams(dimension_semantics=("arbitrary",),
                                             vmem_limit_bytes=VMEM_LIMIT_BYTES),
        name=name,
    )(w)


def _const_spec(shape):
    zeros = (0,) * len(shape)
    return pl.BlockSpec(shape, lambda *_: zeros, pipeline_mode=pl.Buffered(1))


def _mix_call(x, hist_in, ws, bs, lw, *, rows, chained, mask_ws, emit_v, pipelined):
    n = x.shape[0]
    tiles = n // rows
    streams = hist_in.shape[0]
    seg = rows // SEGMENTS
    assert streams == (1 if chained else SEGMENTS) and (seg >= HIST or not chained)
    assert pipelined or tiles == 1
    n_slabs = D_B // LANES
    vec = lambda a: a.reshape(1, -1)
    params = [hist_in, vec(lw["g_mix"]), lw["w_in"], vec(lw["b_in"]), vec(lw["g_v"]), vec(lw["b_v"]),
              ws, bs, lw["w_dw"], vec(lw["b_dw"]), vec(lw["g_c"]), vec(lw["b_c"]),
              vec(lw["g_oa"]), vec(lw["g_ob"]), lw["w_out"]]
    assert len(params) == N_MIX_PARAMS
    scratch = [pltpu.VMEM((n_slabs, SEGMENTS * _pitch(HIST + seg), LANES), f32),
               pltpu.VMEM((n_slabs, (seg + CONV_WIDTH - 1) * SEGMENTS, LANES), f32),
               pltpu.VMEM((n_slabs, SEGMENTS * _pitch(seg), LANES), f32)]
    if pipelined:
        grid = (tiles + 2,)
        rows_of = lambda cols, idx: pl.BlockSpec((rows, cols), lambda s: (idx(s), 0))
        cur = lambda s: jnp.minimum(s, tiles - 1)
        done = lambda s: jnp.maximum(s - 2, 0)
        operands = [x, x] + params
        in_specs = [rows_of(D_MODEL, cur), rows_of(D_MODEL, done)]
        out_rows = lambda cols: rows_of(cols, done)
        z_buf = pltpu.VMEM((rows, 2 * D_A + 2 * D_B), f32)
        y_buf = pltpu.VMEM((rows, D_A + D_B), bf16)
        scratch = [z_buf, z_buf, y_buf, y_buf, pltpu.VMEM((rows, D_MODEL), bf16)] + scratch
    else:
        grid = (tiles,)
        operands = [x] + params
        out_rows = lambda cols: pl.BlockSpec((rows, cols), lambda s: (s, 0))
        in_specs = [out_rows(D_MODEL)]
    in_specs += [_const_spec(a.shape) for a in params]
    out_shape = [jax.ShapeDtypeStruct((n, D_MODEL), f32),
                 jax.ShapeDtypeStruct((streams, HIST, D_B), f32)]
    out_specs = [out_rows(D_MODEL), _const_spec((streams, HIST, D_B))]
    if emit_v:
        out_shape.append(jax.ShapeDtypeStruct((n, D_A), f32))
        out_specs.append(out_rows(D_A))
    return pl.pallas_call(
        functools.partial(_mix_kernel, chained=chained, mask_ws=mask_ws, emit_v=emit_v, pipelined=pipelined),
        grid=grid,
        in_specs=in_specs,
        out_specs=out_specs,
        out_shape=out_shape,
        scratch_shapes=scratch,
        compiler_params=pltpu.CompilerParams(dimension_semantics=("arbitrary",),
                                             vmem_limit_bytes=VMEM_LIMIT_BYTES),
        name="mix_prompt" if pipelined else "mix_sample",
    )(*operands)


def _ffn_kernel(x_ref, g_ref, w1_ref, w2_ref, o_ref, hn_ref):
    @pl.when(pl.program_id(1) == 0)
    def _():
        x = x_ref[...]
        hn_ref[...] = _rms(x, g_ref[...]).astype(bf16)
        o_ref[...] = x

    h = jnp.dot(hn_ref[...], pltpu.bitcast(w1_ref[...], bf16), preferred_element_type=f32)
    h = jnp.square(jnp.maximum(h, 0.0)).astype(bf16)
    o_ref[...] += jnp.dot(h, pltpu.bitcast(w2_ref[...], bf16), preferred_element_type=f32)


def _ffn_call(x, g_ffn, w1, w2, *, rows, name):
    n = x.shape[0]
    return pl.pallas_call(
        _ffn_kernel,
        grid=(n // rows, D_FF // FFN_COLS),
        in_specs=[pl.BlockSpec((rows, D_MODEL), lambda i, f: (i, 0)),
                  _const_spec((1, D_MODEL)),
                  pl.BlockSpec((D_MODEL // 2, FFN_COLS), lambda i, f: (0, f)),
                  pl.BlockSpec((FFN_COLS // 2, D_MODEL), lambda i, f: (f, 0))],
        out_specs=pl.BlockSpec((rows, D_MODEL), lambda i, f: (i, 0)),
        out_shape=jax.ShapeDtypeStruct((n, D_MODEL), f32),
        scratch_shapes=[pltpu.VMEM((rows, D_MODEL), bf16)],
        compiler_params=pltpu.CompilerParams(dimension_semantics=("arbitrary", "arbitrary"),
                                             vmem_limit_bytes=VMEM_LIMIT_BYTES),
        name=name,
    )(x, g_ffn.reshape(1, -1), w1, w2)


def _ple_kernel(x_ref, p_ref, g_ple_ref, w_pg_ref, w_pe_ref, g_final_ref, o_ref):
    x = x_ref[...]
    gate = jax.nn.sigmoid(jnp.dot(_rms(x, g_ple_ref[...]).astype(bf16), pltpu.bitcast(w_pg_ref[...], bf16),
                                  preferred_element_type=f32))
    emb = jnp.dot(p_ref[...].astype(bf16), pltpu.bitcast(w_pe_ref[...], bf16), preferred_element_type=f32)
    o_ref[...] = _rms(x + emb * gate, g_final_ref[...])


def _ple_call(x, p, g_ple, w_pg, w_pe, g_final, *, rows, name):
    n = x.shape[0]
    return pl.pallas_call(
        _ple_kernel,
        grid=(n // rows,),
        in_specs=[pl.BlockSpec((rows, D_MODEL), lambda i: (i, 0)),
                  pl.BlockSpec((rows, D_PLE), lambda i: (i, 0)),
                  _const_spec((1, D_MODEL)),
                  _const_spec((D_MODEL // 2, D_MODEL)),
                  _const_spec((D_PLE // 2, D_MODEL)),
                  _const_spec((1, D_MODEL))],
        out_specs=pl.BlockSpec((rows, D_MODEL), lambda i: (i, 0)),
        out_shape=jax.ShapeDtypeStruct((n, D_MODEL), f32),
        compiler_params=pltpu.CompilerParams(dimension_semantics=("arbitrary",),
                                             vmem_limit_bytes=VMEM_LIMIT_BYTES),
        name=name,
    )(x, p, g_ple.reshape(1, -1), w_pg, w_pe, g_final.reshape(1, -1))


def kernel(x_prompt, x_sample, p_prompt, p_sample, cache_conv, g_mix, w_in, b_in, g_v, b_v, w_s, b_s, w_dw, b_dw, g_c, b_c, g_oa, g_ob, w_out, g_ffn, w1, w2, g_ple, w_pg, w_pe, g_final):
    depth, batch, seq, _ = p_prompt.shape
    _, dec_batch, dec_seq, _ = p_sample.shape
    assert depth == 1 and batch == 1
    assert seq % MIX_ROWS == 0 and seq % FFN_ROWS == 0 and seq % PLE_ROWS == 0
    assert dec_batch == SEGMENTS and dec_batch * dec_seq == MLP_CHUNK and dec_seq % SUBLANES == 0
    assert dec_seq <= CHUNK

    lw = dict(g_mix=g_mix[0], w_in=_pack_rows(w_in[0], "pack_w_in"), b_in=b_in[0], g_v=g_v[0], b_v=b_v[0],
              w_dw=w_dw[0], b_dw=b_dw[0], g_c=g_c[0], b_c=b_c[0], g_oa=g_oa[0], g_ob=g_ob[0],
              w_out=_pack_rows(w_out[0], "pack_w_out"))
    w1b, w2b = _pack_rows(w1[0], "pack_w1"), _pack_rows(w2[0], "pack_w2")
    w_pgb, w_peb = _pack_rows(w_pg[0], "pack_w_pg"), _pack_rows(w_pe[0], "pack_w_pe")
    ws, bs = w_s[0], b_s[0]

    bs_prompt = jnp.repeat(bs.T, HEAD_DIM_A, axis=1)
    hist0 = jnp.zeros((1, HIST, D_B), f32)
    xp, hist_p = _mix_call(x_prompt.reshape(seq, D_MODEL), hist0, ws, bs_prompt, lw,
                           rows=MIX_ROWS, chained=True, mask_ws=True, emit_v=False, pipelined=True)
    xp = _ffn_call(xp, g_ffn[0], w1b, w2b, rows=FFN_ROWS, name="ffn_prompt")
    yp = _ple_call(xp, p_prompt.reshape(seq, D_PLE), g_ple[0], w_pgb, w_peb, g_final,
                   rows=PLE_ROWS, name="ple_prompt")

    qi = jnp.arange(MLP_CHUNK)
    keep = (qi[None, :] // CHUNK) <= (qi[:, None] // CHUNK)
    ws_head = jnp.where(keep[None], ws, jnp.zeros_like(ws))[:, :dec_seq, :dec_seq]
    ws_sample = jnp.einsum("st,hij->hsitj", jnp.eye(dec_batch, dtype=f32), ws_head)
    ws_sample = ws_sample.reshape(N_HEADS_A, MLP_CHUNK, MLP_CHUNK)
    bs_sample = jnp.tile(jnp.repeat(bs[:, :dec_seq].T, HEAD_DIM_A, axis=1), (dec_batch, 1))
    hist_s0 = jnp.pad(cache_conv[0], ((0, 0), (HIST_PAD, 0), (0, 0)))
    rows_s = dec_batch * dec_seq
    xs, hist_s, v_s = _mix_call(x_sample.reshape(rows_s, D_MODEL), hist_s0, ws_sample, bs_sample, lw,
                                rows=rows_s, chained=False, mask_ws=False, emit_v=True, pipelined=False)
    xs = _ffn_call(xs, g_ffn[0], w1b, w2b, rows=rows_s, name="ffn_sample")
    ys = _ple_call(xs, p_sample.reshape(rows_s, D_PLE), g_ple[0], w_pgb, w_peb, g_final,
                   rows=rows_s, name="ple_sample")

    return (yp.reshape(batch, seq, D_MODEL),
            ys.reshape(dec_batch, dec_seq, D_MODEL),
            hist_p[:, HIST_PAD:, :][None],
            hist_s[:, HIST_PAD:, :][None],
            v_s.reshape(1, dec_batch, dec_seq, D_A))
```

```python
import functools

import jax
import jax.numpy as jnp
from jax import lax
from jax.experimental import pallas as pl
from jax.experimental.pallas import tpu as pltpu

D_MODEL = 2048
D_A = 1024
D_B = 1024
N_HEADS_A = 8
HEAD_DIM_A = D_A // N_HEADS_A
MLP_CHUNK = 128
CHUNK = 64
CONV_WIDTH = 31
D_FF = 4 * D_MODEL
D_PLE = 256
EPS = 1e-6

SUBLANES = 8
LANES = 128
BF16_SUBLANES = 16
SEGMENTS = SUBLANES
HIST = 32
HIST_PAD = HIST - (CONV_WIDTH - 1)
VMEM_LIMIT_BYTES = 60 * 1024 * 1024

MIX_ROWS = 256
FFN_ROWS = 1024
FFN_COLS = 1024
PLE_ROWS = 512
PACK_BLOCK_BYTES = 4 * 1024 * 1024
IN_PROJ_PIECES = 8
OUT_PROJ_PIECES = 4
N_MIX_PARAMS = 15

f32 = jnp.float32
bf16 = jnp.bfloat16


def _rms(x, g):
    return x * lax.rsqrt(jnp.mean(x * x, axis=-1, keepdims=True) + EPS) * g


def _ln(x, g, b):
    mu = jnp.mean(x, axis=-1, keepdims=True)
    xc = x - mu
    var = jnp.mean(xc * xc, axis=-1, keepdims=True)
    return xc * lax.rsqrt(var + EPS) * g + b


def _pitch(n):
    return ((-(-n // SUBLANES)) | 1) * SUBLANES


_SLABS = [slice(l * LANES, (l + 1) * LANES) for l in range(D_B // LANES)]


def _load_conv_history(hist_in_ref, stage_ref, *, seg, chained):
    pin = _pitch(HIST + seg)
    for r in range(1 if chained else SEGMENTS):
        for l, lanes in enumerate(_SLABS):
            stage_ref[l, r * pin:r * pin + HIST, :] = hist_in_ref[r, :, lanes]


def _stage_conv_taps(glu, stage_ref, taps_ref, *, chained):
    rows = glu.shape[0]
    seg = rows // SEGMENTS
    pin = _pitch(HIST + seg)
    for r in range(SEGMENTS):
        for l, lanes in enumerate(_SLABS):
            stage_ref[l, r * pin + HIST:r * pin + HIST + seg, :] = glu[r * seg:(r + 1) * seg, lanes]
            if chained and r > 0:
                stage_ref[l, r * pin:r * pin + HIST, :] = glu[r * seg - HIST:r * seg, lanes]
    for q in range(seg + CONV_WIDTH - 1):
        for l in range(len(_SLABS)):
            taps_ref[l, q * SEGMENTS:(q + 1) * SEGMENTS, :] = (
                stage_ref[l, pl.ds(HIST_PAD + q, SEGMENTS, stride=pin), :])
    if chained:
        for l, lanes in enumerate(_SLABS):
            stage_ref[l, 0:HIST, :] = glu[rows - HIST:, lanes]


def _conv_slab(l, rows, w_dw_ref, b_dw_ref, taps_ref, unperm_ref, after):
    seg = rows // SEGMENTS
    pout = _pitch(seg)
    lanes = _SLABS[l]
    c = jnp.broadcast_to(after(b_dw_ref[:, lanes]), (rows, LANES))
    for k in range(CONV_WIDTH):
        c = c + after(w_dw_ref[k:k + 1, lanes]) * taps_ref[l, k * SEGMENTS:k * SEGMENTS + rows, :]
    for b in range(seg):
        unperm_ref[l, pl.ds(b, SEGMENTS, stride=pout), :] = c[b * SEGMENTS:(b + 1) * SEGMENTS, :]
    return jnp.concatenate([unperm_ref[l, r * pout:r * pout + seg, :] for r in range(SEGMENTS)], axis=0)


def _new_conv_history(glu, hist_in_ref, *, chained):
    rows = glu.shape[0]
    seg = rows // SEGMENTS
    if chained:
        return glu[rows - HIST:, :][None]
    if seg >= HIST:
        return jnp.stack([glu[(r + 1) * seg - HIST:(r + 1) * seg, :] for r in range(SEGMENTS)])
    return jnp.stack([jnp.concatenate([hist_in_ref[r, seg:, :], glu[r * seg:(r + 1) * seg, :]], axis=0)
                      for r in range(SEGMENTS)])


def _zero_of(val):
    last = val[val.shape[0] - SUBLANES:, val.shape[1] - LANES:]
    return jnp.where(jnp.isfinite(last), last, 0.0) * 0.0


def _after(tie, row):
    return row if tie is None else row + jnp.tile(tie, (1, row.shape[1] // LANES))


class _Interleave:
    def __init__(self, pieces):
        self._pieces = list(pieces)
        self._tie = None

    def next(self, stage_out=None):
        prev = self._tie
        self._tie = None
        if self._pieces:
            offset = None
            if stage_out is not None:
                offset = pl.multiple_of(_zero_of(stage_out).astype(jnp.int32)[0, 0], BF16_SUBLANES)
            self._tie = _zero_of(self._pieces.pop(0)(offset))[0:1, :]
        return prev

    def flush(self):
        while self._pieces:
            self._pieces.pop(0)(None)


def _mixers(z_of, sched, hist_in_ref, g_v_ref, b_v_ref, ws_ref, bs_ref, w_dw_ref, b_dw_ref, g_c_ref, b_c_ref,
            g_oa_ref, g_ob_ref, hist_out_ref, v_ref, stage_ref, taps_ref, unperm_ref, *, rows, chained, mask_ws,
            first_stage=None):
    tie = sched.next()
    if first_stage is not None:
        first_stage()
    u = jax.nn.gelu(z_of(0, D_A, tie))
    tie = sched.next(u)
    v = _ln(jax.nn.gelu(z_of(D_A, 2 * D_A, tie)), g_v_ref[...], b_v_ref[...])
    if v_ref is not None:
        v_ref[...] = v
    ws = ws_ref[...]
    if mask_ws:
        qi = lax.broadcasted_iota(jnp.int32, (MLP_CHUNK, MLP_CHUNK), 0)
        ki = lax.broadcasted_iota(jnp.int32, (MLP_CHUNK, MLP_CHUNK), 1)
        keep = (ki // CHUNK) <= (qi // CHUNK)
        ws = jnp.where(keep[None], ws, jnp.zeros_like(ws))
    ws = ws.astype(bf16)
    vb = v.astype(bf16)
    mixed_chunks = []
    for c in range(rows // MLP_CHUNK):
        r0 = c * MLP_CHUNK
        heads = [
            jnp.dot(ws[hd], vb[r0:r0 + MLP_CHUNK, hd * HEAD_DIM_A:(hd + 1) * HEAD_DIM_A],
                    preferred_element_type=f32)
            for hd in range(N_HEADS_A)
        ]
        mixed_chunks.append(jnp.concatenate(heads, axis=1) + bs_ref[...])
    mixed = mixed_chunks[0] if len(mixed_chunks) == 1 else jnp.concatenate(mixed_chunks, axis=0)
    tie = sched.next(v)
    y_a = _rms(u * mixed, _after(tie, g_oa_ref[...]))

    tie = sched.next(y_a)
    glu = z_of(2 * D_A, 2 * D_A + D_B, tie) * jax.nn.sigmoid(z_of(2 * D_A + D_B, 2 * D_A + 2 * D_B, tie))
    hist_out_ref[...] = _new_conv_history(glu, hist_in_ref, chained=chained)
    _stage_conv_taps(glu, stage_ref, taps_ref, chained=chained)
    conv_slabs = []
    stage_out = glu
    for l in range(len(_SLABS)):
        tie = sched.next(stage_out)
        conv_slabs.append(_conv_slab(l, rows, w_dw_ref, b_dw_ref, taps_ref, unperm_ref,
                                     functools.partial(_after, tie)))
        stage_out = conv_slabs[-1]
    tie = sched.next(stage_out)
    c = jnp.concatenate(conv_slabs, axis=1)
    y_b = _rms(jax.nn.silu(_ln(c, g_c_ref[...], _after(tie, b_c_ref[...]))), g_ob_ref[...])
    return y_a.astype(bf16), y_b.astype(bf16)


def _mix_step(x_ref, x_out_ref, z_ref, z_next_ref, y_ref, y_next_ref, hn_ref, hist_in_ref, g_mix_ref, w_in_ref,
              b_in_ref, g_v_ref, b_v_ref, ws_ref, bs_ref, w_dw_ref, b_dw_ref, g_c_ref, b_c_ref, g_oa_ref,
              g_ob_ref, w_out_ref, o_ref, hist_out_ref, v_ref, stage_ref, taps_ref, unperm_ref, *, chained,
              mask_ws):
    rows = x_ref.shape[0]
    out_cols = D_MODEL // OUT_PROJ_PIECES
    in_cols = (2 * D_A + 2 * D_B) // IN_PROJ_PIECES

    def lhs_of(ref, offset):
        return ref[...] if offset is None else ref[pl.ds(offset, rows), :]

    def out_proj_piece(j, offset):
        cols = slice(j * out_cols, (j + 1) * out_cols)
        oj = jnp.dot(lhs_of(y_ref, offset), pltpu.bitcast(w_out_ref[:, cols], bf16), preferred_element_type=f32)
        o_ref[:, cols] = x_out_ref[:, cols] + oj
        return oj

    def in_proj_piece(j, offset):
        cols = slice(j * in_cols, (j + 1) * in_cols)
        zj = jnp.dot(lhs_of(hn_ref, offset), pltpu.bitcast(w_in_ref[:, cols], bf16), preferred_element_type=f32)
        z_next_ref[:, cols] = zj
        return zj

    def norm_next_tile():
        hn_ref[...] = _rms(x_ref[...], g_mix_ref[...]).astype(bf16)

    sched = _Interleave([functools.partial(out_proj_piece, j) for j in range(OUT_PROJ_PIECES)]
                        + [functools.partial(in_proj_piece, j) for j in range(IN_PROJ_PIECES)])
    z_of = lambda lo, hi, tie: z_ref[:, lo:hi] + _after(tie, b_in_ref[:, lo:hi])
    y_a, y_b = _mixers(z_of, sched, hist_in_ref, g_v_ref, b_v_ref, ws_ref, bs_ref, w_dw_ref, b_dw_ref, g_c_ref,
                       b_c_ref, g_oa_ref, g_ob_ref, hist_out_ref, v_ref, stage_ref, taps_ref, unperm_ref,
                       rows=rows, chained=chained, mask_ws=mask_ws, first_stage=norm_next_tile)
    y_next_ref[:, 0:D_A] = y_a
    y_next_ref[:, D_A:] = y_b
    sched.flush()


def _mix_single(x_ref, hist_in_ref, g_mix_ref, w_in_ref, b_in_ref, g_v_ref, b_v_ref, ws_ref, bs_ref, w_dw_ref,
                b_dw_ref, g_c_ref, b_c_ref, g_oa_ref, g_ob_ref, w_out_ref, o_ref, hist_out_ref, v_ref, stage_ref,
                taps_ref, unperm_ref, *, chained, mask_ws):
    rows = x_ref.shape[0]
    _load_conv_history(hist_in_ref, stage_ref, seg=rows // SEGMENTS, chained=chained)
    x = x_ref[...]
    z = jnp.dot(_rms(x, g_mix_ref[...]).astype(bf16), pltpu.bitcast(w_in_ref[...], bf16),
                preferred_element_type=f32)
    z_of = lambda lo, hi, tie: z[:, lo:hi] + b_in_ref[:, lo:hi]
    y_a, y_b = _mixers(z_of, _Interleave([]), hist_in_ref, g_v_ref, b_v_ref, ws_ref, bs_ref, w_dw_ref, b_dw_ref,
                       g_c_ref, b_c_ref, g_oa_ref, g_ob_ref, hist_out_ref, v_ref, stage_ref, taps_ref, unperm_ref,
                       rows=rows, chained=chained, mask_ws=mask_ws)
    y = jnp.concatenate([y_a, y_b], axis=1)
    o_ref[...] = x + jnp.dot(y, pltpu.bitcast(w_out_ref[...], bf16), preferred_element_type=f32)


def _mix_kernel(*refs, chained, mask_ws, emit_v, pipelined):
    refs = list(refs)
    x_ref = refs.pop(0)
    x_out_ref = refs.pop(0) if pipelined else None
    params = refs[:N_MIX_PARAMS]
    hist_in_ref = params[0]
    outs = refs[N_MIX_PARAMS:]
    o_ref, hist_out_ref = outs[:2]
    v_ref = outs[2] if emit_v else None
    stage_ref, taps_ref, unperm_ref = outs[-3:]
    rows = x_ref.shape[0]
    tail = (o_ref, hist_out_ref, v_ref, stage_ref, taps_ref, unperm_ref)

    if not pipelined:
        _mix_single(x_ref, *params, *tail, chained=chained, mask_ws=mask_ws)
        return

    z0_ref, z1_ref, y0_ref, y1_ref, hn_ref = outs[-8:-3]
    step = pl.program_id(0)
    body = functools.partial(_mix_step, chained=chained, mask_ws=mask_ws)

    @pl.when(step == 0)
    def _():
        z1_ref[...] = jnp.zeros_like(z1_ref)
        y0_ref[...] = jnp.zeros_like(y0_ref)
        stage_ref[...] = jnp.zeros_like(stage_ref)

    @pl.when(step == 1)
    def _():
        _load_conv_history(hist_in_ref, stage_ref, seg=rows // SEGMENTS, chained=chained)

    @pl.when(step % 2 == 0)
    def _():
        body(x_ref, x_out_ref, z1_ref, z0_ref, y0_ref, y1_ref, hn_ref, *params, *tail)

    @pl.when(step % 2 == 1)
    def _():
        body(x_ref, x_out_ref, z0_ref, z1_ref, y1_ref, y0_ref, hn_ref, *params, *tail)


def _pack_kernel(w_ref, o_ref):
    o_ref[...] = pltpu.bitcast(w_ref[...].astype(bf16), jnp.uint32)


def _pack_rows(w, name):
    k, n = w.shape
    rows = max(BF16_SUBLANES, min(k, PACK_BLOCK_BYTES // (4 * n)))
    assert k % rows == 0 and rows % BF16_SUBLANES == 0
    return pl.pallas_call(
        _pack_kernel,
        grid=(k // rows,),
        in_specs=[pl.BlockSpec((rows, n), lambda i: (i, 0))],
        out_specs=pl.BlockSpec((rows // 2, n), lambda i: (i, 0)),
        out_shape=jax.ShapeDtypeStruct((k // 2, n), jnp.uint32),
        compiler_params=pltpu.CompilerParams(dimension_semantics=("arbitrary",),
                                             vmem_limit_bytes=VMEM_LIMIT_BYTES),
        name=name,
    )(w)


def _const_spec(shape):
    zeros = (0,) * len(shape)
    return pl.BlockSpec(shape, lambda *_: zeros, pipeline_mode=pl.Buffered(1))


def _mix_call(x, hist_in, ws, bs, lw, *, rows, chained, mask_ws, emit_v, pipelined):
    n = x.shape[0]
    tiles = n // rows
    streams = hist_in.shape[0]
    seg = rows // SEGMENTS
    assert streams == (1 if chained else SEGMENTS) and (seg >= HIST or not chained)
    assert pipelined or tiles == 1
    n_slabs = D_B // LANES
    vec = lambda a: a.reshape(1, -1)
    params = [hist_in, vec(lw["g_mix"]), lw["w_in"], vec(lw["b_in"]), vec(lw["g_v"]), vec(lw["b_v"]),
              ws, bs, lw["w_dw"], vec(lw["b_dw"]), vec(lw["g_c"]), vec(lw["b_c"]),
              vec(lw["g_oa"]), vec(lw["g_ob"]), lw["w_out"]]
    assert len(params) == N_MIX_PARAMS
    scratch = [pltpu.VMEM((n_slabs, SEGMENTS * _pitch(HIST + seg), LANES), f32),
               pltpu.VMEM((n_slabs, (seg + CONV_WIDTH - 1) * SEGMENTS, LANES), f32),
               pltpu.VMEM((n_slabs, SEGMENTS * _pitch(seg), LANES), f32)]
    if pipelined:
        grid = (tiles + 2,)
        rows_of = lambda cols, idx: pl.BlockSpec((rows, cols), lambda s: (idx(s), 0))
        cur = lambda s: jnp.minimum(s, tiles - 1)
        done = lambda s: jnp.maximum(s - 2, 0)
        operands = [x, x] + params
        in_specs = [rows_of(D_MODEL, cur), rows_of(D_MODEL, done)]
        out_rows = lambda cols: rows_of(cols, done)
        z_buf = pltpu.VMEM((rows, 2 * D_A + 2 * D_B), f32)
        y_buf = pltpu.VMEM((rows, D_A + D_B), bf16)
        scratch = [z_buf, z_buf, y_buf, y_buf, pltpu.VMEM((rows, D_MODEL), bf16)] + scratch
    else:
        grid = (tiles,)
        operands = [x] + params
        out_rows = lambda cols: pl.BlockSpec((rows, cols), lambda s: (s, 0))
        in_specs = [out_rows(D_MODEL)]
    in_specs += [_const_spec(a.shape) for a in params]
    out_shape = [jax.ShapeDtypeStruct((n, D_MODEL), f32),
                 jax.ShapeDtypeStruct((streams, HIST, D_B), f32)]
    out_specs = [out_rows(D_MODEL), _const_spec((streams, HIST, D_B))]
    if emit_v:
        out_shape.append(jax.ShapeDtypeStruct((n, D_A), f32))
        out_specs.append(out_rows(D_A))
    return pl.pallas_call(
        functools.partial(_mix_kernel, chained=chained, mask_ws=mask_ws, emit_v=emit_v, pipelined=pipelined),
        grid=grid,
        in_specs=in_specs,
        out_specs=out_specs,
        out_shape=out_shape,
        scratch_shapes=scratch,
        compiler_params=pltpu.CompilerParams(dimension_semantics=("arbitrary",),
                                             vmem_limit_bytes=VMEM_LIMIT_BYTES),
        name="mix_prompt" if pipelined else "mix_sample",
    )(*operands)


def _ffn_kernel(x_ref, g_ref, w1_ref, w2_ref, o_ref, hn_ref):
    def d_ff_piece():
        h = jnp.dot(hn_ref[...], pltpu.bitcast(w1_ref[...], bf16), preferred_element_type=f32)
        h = jnp.square(jnp.maximum(h, 0.0)).astype(bf16)
        return jnp.dot(h, pltpu.bitcast(w2_ref[...], bf16), preferred_element_type=f32)

    @pl.when(pl.program_id(1) == 0)
    def _():
        x = x_ref[...]
        hn_ref[...] = _rms(x, g_ref[...]).astype(bf16)
        o_ref[...] = x + d_ff_piece()

    @pl.when(pl.program_id(1) > 0)
    def _():
        o_ref[...] += d_ff_piece()


def _ffn_call(x, g_ffn, w1, w2, *, rows, name):
    n = x.shape[0]
    return pl.pallas_call(
        _ffn_kernel,
        grid=(n // rows, D_FF // FFN_COLS),
        in_specs=[pl.BlockSpec((rows, D_MODEL), lambda i, f: (i, 0)),
                  _const_spec((1, D_MODEL)),
                  pl.BlockSpec((D_MODEL // 2, FFN_COLS), lambda i, f: (0, f)),
                  pl.BlockSpec((FFN_COLS // 2, D_MODEL), lambda i, f: (f, 0))],
        out_specs=pl.BlockSpec((rows, D_MODEL), lambda i, f: (i, 0)),
        out_shape=jax.ShapeDtypeStruct((n, D_MODEL), f32),
        scratch_shapes=[pltpu.VMEM((rows, D_MODEL), bf16)],
        compiler_params=pltpu.CompilerParams(dimension_semantics=("arbitrary", "arbitrary"),
                                             vmem_limit_bytes=VMEM_LIMIT_BYTES),
        name=name,
    )(x, g_ffn.reshape(1, -1), w1, w2)


def _ple_kernel(x_ref, p_ref, g_ple_ref, w_pg_ref, w_pe_ref, g_final_ref, o_ref):
    x = x_ref[...]
    gate = jax.nn.sigmoid(jnp.dot(_rms(x, g_ple_ref[...]).astype(bf16), pltpu.bitcast(w_pg_ref[...], bf16),
                                  preferred_element_type=f32))
    emb = jnp.dot(p_ref[...].astype(bf16), pltpu.bitcast(w_pe_ref[...], bf16), preferred_element_type=f32)
    o_ref[...] = _rms(x + emb * gate, g_final_ref[...])


def _ple_call(x, p, g_ple, w_pg, w_pe, g_final, *, rows, name):
    n = x.shape[0]
    return pl.pallas_call(
        _ple_kernel,
        grid=(n // rows,),
        in_specs=[pl.BlockSpec((rows, D_MODEL), lambda i: (i, 0)),
                  pl.BlockSpec((rows, D_PLE), lambda i: (i, 0)),
                  _const_spec((1, D_MODEL)),
                  _const_spec((D_MODEL // 2, D_MODEL)),
                  _const_spec((D_PLE // 2, D_MODEL)),
                  _const_spec((1, D_MODEL))],
        out_specs=pl.BlockSpec((rows, D_MODEL), lambda i: (i, 0)),
        out_shape=jax.ShapeDtypeStruct((n, D_MODEL), f32),
        compiler_params=pltpu.CompilerParams(dimension_semantics=("arbitrary",),
                                             vmem_limit_bytes=VMEM_LIMIT_BYTES),
        name=name,
    )(x, p, g_ple.reshape(1, -1), w_pg, w_pe, g_final.reshape(1, -1))


def kernel(x_prompt, x_sample, p_prompt, p_sample, cache_conv, g_mix, w_in, b_in, g_v, b_v, w_s, b_s, w_dw, b_dw, g_c, b_c, g_oa, g_ob, w_out, g_ffn, w1, w2, g_ple, w_pg, w_pe, g_final):
    depth, batch, seq, _ = p_prompt.shape
    _, dec_batch, dec_seq, _ = p_sample.shape
    assert depth == 1 and batch == 1
    assert seq % MIX_ROWS == 0 and seq % FFN_ROWS == 0 and seq % PLE_ROWS == 0
    assert dec_batch == SEGMENTS and dec_batch * dec_seq == MLP_CHUNK and dec_seq % SUBLANES == 0
    assert dec_seq <= CHUNK

    lw = dict(g_mix=g_mix[0], w_in=_pack_rows(w_in[0], "pack_w_in"), b_in=b_in[0], g_v=g_v[0], b_v=b_v[0],
              w_dw=w_dw[0], b_dw=b_dw[0], g_c=g_c[0], b_c=b_c[0], g_oa=g_oa[0], g_ob=g_ob[0],
              w_out=_pack_rows(w_out[0], "pack_w_out"))
    w1b, w2b = _pack_rows(w1[0], "pack_w1"), _pack_rows(w2[0], "pack_w2")
    w_pgb, w_peb = _pack_rows(w_pg[0], "pack_w_pg"), _pack_rows(w_pe[0], "pack_w_pe")
    ws, bs = w_s[0], b_s[0]

    bs_prompt = jnp.repeat(bs.T, HEAD_DIM_A, axis=1)
    hist0 = jnp.zeros((1, HIST, D_B), f32)
    xp, hist_p = _mix_call(x_prompt.reshape(seq, D_MODEL), hist0, ws, bs_prompt, lw,
                           rows=MIX_ROWS, chained=True, mask_ws=True, emit_v=False, pipelined=True)
    xp = _ffn_call(xp, g_ffn[0], w1b, w2b, rows=FFN_ROWS, name="ffn_prompt")
    yp = _ple_call(xp, p_prompt.reshape(seq, D_PLE), g_ple[0], w_pgb, w_peb, g_final,
                   rows=PLE_ROWS, name="ple_prompt")

    qi = jnp.arange(MLP_CHUNK)
    keep = (qi[None, :] // CHUNK) <= (qi[:, None] // CHUNK)
    ws_head = jnp.where(keep[None], ws, jnp.zeros_like(ws))[:, :dec_seq, :dec_seq]
    ws_sample = jnp.einsum("st,hij->hsitj", jnp.eye(dec_batch, dtype=f32), ws_head)
    ws_sample = ws_sample.reshape(N_HEADS_A, MLP_CHUNK, MLP_CHUNK)
    bs_sample = jnp.tile(jnp.repeat(bs[:, :dec_seq].T, HEAD_DIM_A, axis=1), (dec_batch, 1))
    hist_s0 = jnp.pad(cache_conv[0], ((0, 0), (HIST_PAD, 0), (0, 0)))
    rows_s = dec_batch * dec_seq
    xs, hist_s, v_s = _mix_call(x_sample.reshape(rows_s, D_MODEL), hist_s0, ws_sample, bs_sample, lw,
                                rows=rows_s, chained=False, mask_ws=False, emit_v=True, pipelined=False)
    xs = _ffn_call(xs, g_ffn[0], w1b, w2b, rows=rows_s, name="ffn_sample")
    ys = _ple_call(xs, p_sample.reshape(rows_s, D_PLE), g_ple[0], w_pgb, w_peb, g_final,
                   rows=rows_s, name="ple_sample")

    return (yp.reshape(batch, seq, D_MODEL),
            ys.reshape(dec_batch, dec_seq, D_MODEL),
            hist_p[:, HIST_PAD:, :][None],
            hist_s[:, HIST_PAD:, :][None],
            v_s.reshape(1, dec_batch, dec_seq, D_A))
```
